```python
import math
import jax
import jax.numpy as jnp
from jax import lax
import numpy as np

D_MODEL = 4096
BATCH = 4
SEQ = 2048
DEPTH = 2
DEC_BATCH = 32
DEC_SEQ = 4
PAST_LEN = 16384
PAGE_SIZE = 128

N_A_LAYERS = (DEPTH + 1) // 2
N_B_LAYERS = DEPTH // 2
HEAD_DIM_A = 128
N_HEADS_A = D_MODEL // HEAD_DIM_A
N_KV_A = 8
GROUP_A = N_HEADS_A // N_KV_A
Q_BLOCK = 128
SB_BIAS_INIT = -6.0
HEAD_DIM_B = 64
N_HEADS_B = D_MODEL // HEAD_DIM_B
N_KV_B = 8
GROUP_B = N_HEADS_B // N_KV_B
WINDOW = 128
N_BUCKETS = 32
MAX_DISTANCE = 128
D_FF = 7 * D_MODEL // 2
N_EXPERTS = 8
TOP_K = 2
D_FF_EXPERT = 7 * D_MODEL // 2
MOE_BLOCK = 128
RMS_EPS = 1e-5
F32 = jnp.float32

kernel_name = "hybrid_stickbreak_swa_adaln_moe_step"


def _rmsnorm(x, g):
    xf = x.astype(F32)
    y = xf * lax.rsqrt(jnp.mean(xf * xf, axis=-1, keepdims=True) + RMS_EPS)
    return (y * g.astype(F32)).astype(x.dtype)


def _adaln(c, w, b):
    m = jax.nn.silu(c) @ w + b
    return jnp.split(m[:, None, :], 6, axis=-1)


def _split_qkv(h, w_qkv, n_heads, n_kv, hd):
    b, s, _ = h.shape
    qkv = h @ w_qkv
    q, k, v = jnp.split(qkv, [n_heads * hd, (n_heads + n_kv) * hd], axis=-1)
    return (q.reshape(b, s, n_kv, n_heads // n_kv, hd),
            k.reshape(b, s, n_kv, hd), v.reshape(b, s, n_kv, hd))


def _sb_weights(z, carry, mask=None):
    l = jax.nn.log_sigmoid(-z)
    if mask is not None:
        l = jnp.where(mask, l, 0.0)
    between = lax.cumsum(l, axis=z.ndim - 1, reverse=True) - l + carry[..., None]
    a = jnp.exp(jax.nn.log_sigmoid(z) + between)
    if mask is not None:
        a = jnp.where(mask, a, 0.0)
    return a, carry + l.sum(-1)


def _stickbreak_prompt(h, w_qkv, w_o, sb_bias):
    b, s, _ = h.shape
    q, k, v = _split_qkv(h, w_qkv, N_HEADS_A, N_KV_A, HEAD_DIM_A)
    scale = HEAD_DIM_A ** -0.5
    bias = sb_bias.reshape(N_KV_A, GROUP_A, 1, 1).astype(F32)
    outs = []
    for i in range(s // Q_BLOCK):
        q0, q1 = i * Q_BLOCK, (i + 1) * Q_BLOCK
        z = jnp.einsum("bqkgd,bskd->bkgqs", q[:, q0:q1], k[:, :q1]).astype(F32) * scale + bias
        mask = jnp.arange(q1)[None, :] < (q0 + jnp.arange(Q_BLOCK))[:, None]
        a, _ = _sb_weights(z, jnp.zeros(z.shape[:-1], F32), mask)
        outs.append(jnp.einsum("bkgqs,bskd->bqkgd", a.astype(v.dtype), v[:, :q1],
                               preferred_element_type=F32))
    o = jnp.concatenate(outs, axis=1).reshape(b, s, N_HEADS_A * HEAD_DIM_A).astype(h.dtype)
    page_shape = (b, s // PAGE_SIZE, PAGE_SIZE, N_KV_A, HEAD_DIM_A)
    return o @ w_o, (k.reshape(page_shape), v.reshape(page_shape))


def _stickbreak_sample(h, k_pool, v_pool, layer, page_table, w_qkv, w_o, sb_bias):
    b, t, _ = h.shape
    q, k, v = _split_qkv(h, w_qkv, N_HEADS_A, N_KV_A, HEAD_DIM_A)
    scale = HEAD_DIM_A ** -0.5
    bias = sb_bias.reshape(N_KV_A, GROUP_A, 1, 1).astype(F32)
    z = jnp.einsum("bqkgd,bskd->bkgqs", q, k).astype(F32) * scale + bias
    mask = jnp.arange(t)[None, :] < jnp.arange(t)[:, None]
    a, carry = _sb_weights(z, jnp.zeros(z.shape[:-1], F32), mask)
    out = jnp.einsum("bkgqs,bskd->bkgqd", a.astype(v.dtype), v, preferred_element_type=F32)

    def page_step(state, pages):
        carry, out = state
        kp = k_pool[layer, pages]
        vp = v_pool[layer, pages]
        z = jnp.einsum("bqkgd,bskd->bkgqs", q, kp).astype(F32) * scale + bias
        a, carry = _sb_weights(z, carry)
        out = out + jnp.einsum("bkgqs,bskd->bkgqd", a.astype(vp.dtype), vp,
                               preferred_element_type=F32)
        return (carry, out), None

    (_, out), _ = lax.scan(page_step, (carry, out), page_table.T, reverse=True)
    o = out.transpose(0, 3, 1, 2, 4).reshape(b, t, N_HEADS_A * HEAD_DIM_A).astype(h.dtype)
    return o @ w_o, (k, v)


def _t5_bucket(dist):
    max_exact = N_BUCKETS // 2
    far = max_exact + (jnp.log(jnp.maximum(dist, 1).astype(F32) / max_exact)
                       / math.log(MAX_DISTANCE / max_exact)
                       * (N_BUCKETS - max_exact)).astype(jnp.int32)
    return jnp.where(dist < max_exact, dist, jnp.minimum(far, N_BUCKETS - 1))


def _band_bias(rel_bias, n_q, n_k, offset):
    dist = offset + jnp.arange(n_q)[:, None] - jnp.arange(n_k)[None, :]
    valid = (dist >= 0) & (dist < WINDOW)
    bias = rel_bias[_t5_bucket(jnp.maximum(dist, 0))]
    bias = bias.transpose(2, 0, 1).reshape(N_KV_B, GROUP_B, n_q, n_k).astype(F32)
    return bias, valid


def _sink_softmax(z, sink):
    m = jnp.maximum(z.max(-1), sink)
    e = jnp.exp(z - m[..., None])
    return e / (e.sum(-1) + jnp.exp(sink - m))[..., None]


def _swa_prompt(h, w_qkv, w_o, sinks, rel_bias):
    b, s, _ = h.shape
    q, k, v = _split_qkv(h, w_qkv, N_HEADS_B, N_KV_B, HEAD_DIM_B)
    nb = s // WINDOW
    qb = q.reshape(b, nb, WINDOW, N_KV_B, GROUP_B, HEAD_DIM_B)

    def band(x):
        xb = x.reshape(b, nb, WINDOW, N_KV_B, HEAD_DIM_B)
        prev = jnp.concatenate([jnp.zeros_like(xb[:, :1]), xb[:, :-1]], axis=1)
        return jnp.concatenate([prev, xb], axis=2)

    kk, vv = band(k), band(v)
    bias, valid = _band_bias(rel_bias, WINDOW, 2 * WINDOW, WINDOW)
    first = (jnp.arange(nb)[:, None, None] == 0) & (jnp.arange(2 * WINDOW)[None, None, :] < WINDOW)
    mask = (valid[None] & ~first)[None, :, None, None]
    z = jnp.einsum("bnqkgd,bnskd->bnkgqs", qb, kk).astype(F32) * HEAD_DIM_B ** -0.5 + bias
    p = _sink_softmax(jnp.where(mask, z, -jnp.inf),
                      sinks.reshape(N_KV_B, GROUP_B, 1).astype(F32))
    o = jnp.einsum("bnkgqs,bnskd->bnqkgd", p.astype(vv.dtype), vv, preferred_element_type=F32)
    o = o.reshape(b, s, N_HEADS_B * HEAD_DIM_B).astype(h.dtype)
    n_keep = min(WINDOW, s)
    return o @ w_o, (k[:, s - n_keep:], v[:, s - n_keep:])


def _swa_sample(h, k_buf, v_buf, w_qkv, w_o, sinks, rel_bias):
    b, t, _ = h.shape
    n_buf = k_buf.shape[1]
    q, k, v = _split_qkv(h, w_qkv, N_HEADS_B, N_KV_B, HEAD_DIM_B)
    kk = jnp.concatenate([k_buf, k.astype(k_buf.dtype)], axis=1)
    vv = jnp.concatenate([v_buf, v.astype(v_buf.dtype)], axis=1)
    bias, valid = _band_bias(rel_bias, t, n_buf + t, n_buf)
    z = jnp.einsum("bqkgd,bskd->bkgqs", q, kk).astype(F32) * HEAD_DIM_B ** -0.5 + bias
    p = _sink_softmax(jnp.where(valid, z, -jnp.inf),
                      sinks.reshape(N_KV_B, GROUP_B, 1).astype(F32))
    o = jnp.einsum("bkgqs,bskd->bqkgd", p.astype(vv.dtype), vv, preferred_element_type=F32)
    o = o.reshape(b, t, N_HEADS_B * HEAD_DIM_B).astype(h.dtype)
    return o @ w_o, (kk[:, t:], vv[:, t:])


def _swiglu(x, wg, wu, wd):
    return (jax.nn.silu(x @ wg) * (x @ wu)) @ wd


def _moe(h, layer, w_router, b_router, w_gate_e, w_up_e, w_down_e):
    b, s, d = h.shape
    n = b * s
    x = h.reshape(n, d)
    logits = (x @ w_router[layer]).astype(F32) + b_router[layer].astype(F32)
    top_val, top_idx = lax.top_k(logits, TOP_K)
    gate = jax.nn.softmax(top_val, axis=-1)
    flat_e = top_idx.reshape(-1)
    flat_tok = jnp.repeat(jnp.arange(n, dtype=jnp.int32), TOP_K)
    flat_g = gate.reshape(-1)
    order = jnp.argsort(flat_e)
    se, stok, sg = flat_e[order], flat_tok[order], flat_g[order]
    counts = jnp.bincount(flat_e, length=N_EXPERTS)
    padded = (counts + MOE_BLOCK - 1) // MOE_BLOCK * MOE_BLOCK
    pad_end = jnp.cumsum(padded)
    rank = jnp.arange(n * TOP_K) - (jnp.cumsum(counts) - counts)[se]
    dest = (pad_end - padded)[se] + rank
    n_slots = -(-(n * TOP_K) // MOE_BLOCK) * MOE_BLOCK + N_EXPERTS * MOE_BLOCK
    n_blocks = n_slots // MOE_BLOCK
    slot_tok = jnp.zeros((n_slots,), jnp.int32).at[dest].set(stok)
    slot_gate = jnp.zeros((n_slots,), F32).at[dest].set(sg)
    block_expert = jnp.minimum(
        jnp.searchsorted(pad_end, jnp.arange(n_blocks) * MOE_BLOCK, side="right"), N_EXPERTS - 1)

    def expert_block(args):
        toks, e = args
        xb = x[toks]
        hb = jax.nn.silu(xb @ w_gate_e[layer, e]) * (xb @ w_up_e[layer, e])
        return hb @ w_down_e[layer, e]

    yb = lax.map(expert_block, (slot_tok.reshape(n_blocks, MOE_BLOCK), block_expert))
    y = jax.ops.segment_sum(yb.reshape(n_slots, d) * slot_gate[:, None].astype(yb.dtype),
                            slot_tok, num_segments=n)
    return y.reshape(b, s, d)


def _trunk(x, c, token_mix, g_mix, g_ffn, g_final, w_ada, b_ada, w_gate_d, w_up_d, w_down_d,
           w_router, b_router, w_gate_e, w_up_e, w_down_e):
    states = []
    for i in range(DEPTH):
        sh_m, sc_m, gt_m, sh_f, sc_f, gt_f = _adaln(c, w_ada[i], b_ada[i])
        o, st = token_mix(i, _rmsnorm(x, g_mix[i]) * (1 + sc_m) + sh_m)
        states.append(st)
        x = x + gt_m * o
        hf = _rmsnorm(x, g_ffn[i]) * (1 + sc_f) + sh_f
        j = i // 2
        if i % 2 == 0:
            f = _swiglu(hf, w_gate_d[j], w_up_d[j], w_down_d[j])
        else:
            f = _moe(hf, j, w_router, b_router, w_gate_e, w_up_e, w_down_e)
        x = x + gt_f * f
    return _rmsnorm(x, g_final), states


def setup_inputs(seed: int = 0) -> dict:
    key = jax.random.key(seed)
    ks = jax.random.split(key, 32)
    n_pages = PAST_LEN // PAGE_SIZE
    n_used = DEC_BATCH * n_pages
    n_phys = n_used + max(1, n_used // 4)
    n_buf = min(WINDOW, PAST_LEN)
    qkv_a = (N_HEADS_A + 2 * N_KV_A) * HEAD_DIM_A
    qkv_b = (N_HEADS_B + 2 * N_KV_B) * HEAD_DIM_B
    sd = D_MODEL ** -0.5

    def nrm(k, shape, scale=1.0):
        return jax.random.normal(k, shape, F32) * scale

    page_table = jax.random.permutation(ks[6], n_phys)[:n_used].reshape(DEC_BATCH, n_pages).astype(jnp.int32)
    return {
        "x_prompt": nrm(ks[0], (BATCH, SEQ, D_MODEL)),
        "x_sample": nrm(ks[1], (DEC_BATCH, DEC_SEQ, D_MODEL)),
        "c_prompt": nrm(ks[2], (BATCH, D_MODEL)),
        "c_sample": nrm(ks[3], (DEC_BATCH, D_MODEL)),
        "cache_k_a": nrm(ks[4], (N_A_LAYERS, n_phys, PAGE_SIZE, N_KV_A, HEAD_DIM_A)),
        "cache_v_a": nrm(ks[5], (N_A_LAYERS, n_phys, PAGE_SIZE, N_KV_A, HEAD_DIM_A)),
        "page_table": page_table,
        "cache_k_b": nrm(ks[7], (N_B_LAYERS, DEC_BATCH, n_buf, N_KV_B, HEAD_DIM_B)),
        "cache_v_b": nrm(ks[8], (N_B_LAYERS, DEC_BATCH, n_buf, N_KV_B, HEAD_DIM_B)),
        "g_mix": 1.0 + nrm(ks[9], (DEPTH, D_MODEL), 0.05),
        "g_ffn": 1.0 + nrm(ks[10], (DEPTH, D_MODEL), 0.05),
        "g_final": 1.0 + nrm(ks[11], (D_MODEL,), 0.05),
        "w_ada": nrm(ks[12], (DEPTH, D_MODEL, 6 * D_MODEL), 0.5 * sd),
        "b_ada": nrm(ks[13], (DEPTH, 6 * D_MODEL), 0.02),
        "w_qkv_a": nrm(ks[14], (N_A_LAYERS, D_MODEL, qkv_a), sd),
        "w_o_a": nrm(ks[15], (N_A_LAYERS, N_HEADS_A * HEAD_DIM_A, D_MODEL), (N_HEADS_A * HEAD_DIM_A) ** -0.5),
        "sb_bias_a": SB_BIAS_INIT + nrm(ks[28], (N_A_LAYERS, N_HEADS_A), 0.1),
        "w_qkv_b": nrm(ks[16], (N_B_LAYERS, D_MODEL, qkv_b), sd),
        "w_o_b": nrm(ks[17], (N_B_LAYERS, N_HEADS_B * HEAD_DIM_B, D_MODEL), (N_HEADS_B * HEAD_DIM_B) ** -0.5),
        "sinks_b": nrm(ks[18], (N_B_LAYERS, N_HEADS_B), 0.5),
        "rel_bias": nrm(ks[19], (N_BUCKETS, N_HEADS_B), 0.5),
        "w_gate_d": nrm(ks[20], (N_A_LAYERS, D_MODEL, D_FF), sd),
        "w_up_d": nrm(ks[21], (N_A_LAYERS, D_MODEL, D_FF), sd),
        "w_down_d": nrm(ks[22], (N_A_LAYERS, D_FF, D_MODEL), D_FF ** -0.5),
        "w_router": nrm(ks[23], (N_B_LAYERS, D_MODEL, N_EXPERTS), sd),
        "b_router": nrm(ks[24], (N_B_LAYERS, N_EXPERTS), 0.01),
        "w_gate_e": nrm(ks[25], (N_B_LAYERS, N_EXPERTS, D_MODEL, D_FF_EXPERT), sd),
        "w_up_e": nrm(ks[26], (N_B_LAYERS, N_EXPERTS, D_MODEL, D_FF_EXPERT), sd),
        "w_down_e": nrm(ks[27], (N_B_LAYERS, N_EXPERTS, D_FF_EXPERT, D_MODEL), D_FF_EXPERT ** -0.5),
    }


def reference(x_prompt, x_sample, c_prompt, c_sample, cache_k_a, cache_v_a, page_table,
              cache_k_b, cache_v_b, g_mix, g_ffn, g_final, w_ada, b_ada, w_qkv_a, w_o_a,
              sb_bias_a, w_qkv_b, w_o_b, sinks_b, rel_bias, w_gate_d, w_up_d, w_down_d,
              w_router, b_router, w_gate_e, w_up_e, w_down_e):
    def mix_prompt(i, h):
        j = i // 2
        if i % 2 == 0:
            return _stickbreak_prompt(h, w_qkv_a[j], w_o_a[j], sb_bias_a[j])
        return _swa_prompt(h, w_qkv_b[j], w_o_b[j], sinks_b[j], rel_bias)

    def mix_sample(i, h):
        j = i // 2
        if i % 2 == 0:
            return _stickbreak_sample(h, cache_k_a, cache_v_a, j, page_table, w_qkv_a[j], w_o_a[j],
                                      sb_bias_a[j])
        return _swa_sample(h, cache_k_b[j], cache_v_b[j], w_qkv_b[j], w_o_b[j], sinks_b[j], rel_bias)

    y_prompt, st_p = _trunk(x_prompt, c_prompt, mix_prompt, g_mix, g_ffn, g_final, w_ada, b_ada,
                            w_gate_d, w_up_d, w_down_d, w_router, b_router, w_gate_e, w_up_e, w_down_e)
    y_sample, st_s = _trunk(x_sample, c_sample, mix_sample, g_mix, g_ffn, g_final, w_ada, b_ada,
                            w_gate_d, w_up_d, w_down_d, w_router, b_router, w_gate_e, w_up_e, w_down_e)
    k_a_prompt = jnp.stack([st_p[i][0] for i in range(0, DEPTH, 2)])
    v_a_prompt = jnp.stack([st_p[i][1] for i in range(0, DEPTH, 2)])
    k_a_sample = jnp.stack([st_s[i][0] for i in range(0, DEPTH, 2)])
    v_a_sample = jnp.stack([st_s[i][1] for i in range(0, DEPTH, 2)])
    k_b_prompt = jnp.stack([st_p[i][0] for i in range(1, DEPTH, 2)])
    v_b_prompt = jnp.stack([st_p[i][1] for i in range(1, DEPTH, 2)])
    k_b_sample = jnp.stack([st_s[i][0] for i in range(1, DEPTH, 2)])
    v_b_sample = jnp.stack([st_s[i][1] for i in range(1, DEPTH, 2)])
    return (y_prompt, y_sample, k_a_prompt, v_a_prompt, k_a_sample, v_a_sample,
            k_b_prompt, v_b_prompt, k_b_sample, v_b_sample)
```

```python
import functools
import math

import numpy as np
import jax
import jax.numpy as jnp
from jax import lax
from jax.experimental import pallas as pl
from jax.experimental.pallas import tpu as pltpu

F32 = jnp.float32
BF16 = jnp.bfloat16

HEAD_DIM_A = 128
N_KV_A = 8
HEAD_DIM_B = 64
N_KV_B = 8
PAGE_SIZE = 128
WINDOW = 128
N_BUCKETS = 32
MAX_DISTANCE = 128
TOP_K = 2
RMS_EPS = 1e-5

LANES = 128
V7X_VMEM_BYTES = 64 * 1024 * 1024
VMEM_LIMIT = V7X_VMEM_BYTES - 8 * 1024 * 1024
CAST_ROWS = 256


def _params(n_axes):
    return pltpu.CompilerParams(dimension_semantics=("arbitrary",) * n_axes,
                                vmem_limit_bytes=VMEM_LIMIT)


def _gmm_kernel(be_ref, nu_ref, a_ref, *refs, n_w, pre_silu):
    w_refs = refs[:n_w]
    o_ref = refs[n_w]
    wb_refs = refs[n_w + 1:]
    r = pl.program_id(2)
    e = be_ref[r]
    prev = be_ref[jnp.maximum(r - 1, 0)]

    @pl.when(jnp.logical_or(r == 0, e != prev))
    def _convert_weights():
        kc = wb_refs[0].shape[0]

        def body(i, c):
            rows = pl.ds(pl.multiple_of(i * CAST_ROWS, CAST_ROWS), CAST_ROWS)
            for w_ref, wb_ref in zip(w_refs, wb_refs):
                wb_ref[rows, :] = w_ref[0, rows, :].astype(BF16)
            return c

        lax.fori_loop(0, kc // CAST_ROWS, body, 0)

    @pl.when(r < nu_ref[0])
    def _compute():
        a = a_ref[...]
        if pre_silu:
            a = a.astype(F32)
            a = a * (1.0 / (1.0 + jnp.exp(-a)))
        a = a.astype(BF16)
        if n_w == 2:
            g = jnp.dot(a, wb_refs[0][...], preferred_element_type=F32)
            u = jnp.dot(a, wb_refs[1][...], preferred_element_type=F32)
            o_ref[0] = (g * (1.0 / (1.0 + jnp.exp(-g))) * u).astype(o_ref.dtype)
        else:
            o_ref[0] = jnp.dot(a, wb_refs[0][...], preferred_element_type=F32).astype(o_ref.dtype)

    @pl.when(r >= nu_ref[0])
    def _unused_block():
        o_ref[...] = jnp.zeros_like(o_ref)


def _gmm(a, ws, blk_e, n_used, *, bm, bn, kc, out_dtype, pre_silu=False, name="gmm"):
    m, k = a.shape
    _, _, n = ws[0].shape
    assert m % bm == 0 and n % bn == 0 and k % kc == 0 and kc % CAST_ROWS == 0
    grid = (k // kc, n // bn, m // bm)
    in_specs = [pl.BlockSpec((bm, kc), lambda kh, j, r, be, nu: (r, kh))]
    in_specs += [pl.BlockSpec((1, kc, bn), lambda kh, j, r, be, nu: (be[r], kh, j)) for _ in ws]
    out_spec = pl.BlockSpec((1, bm, bn), lambda kh, j, r, be, nu: (kh, r, j))
    return pl.pallas_call(
        functools.partial(_gmm_kernel, n_w=len(ws), pre_silu=pre_silu),
        grid_spec=pltpu.PrefetchScalarGridSpec(
            num_scalar_prefetch=2, grid=grid, in_specs=in_specs, out_specs=out_spec,
            scratch_shapes=[pltpu.VMEM((kc, bn), BF16) for _ in ws]),
        out_shape=jax.ShapeDtypeStruct((k // kc, m, n), out_dtype),
        compiler_params=_params(3), name=name,
    )(blk_e, n_used, a, *ws)


def _dense_mm(a, ws, *, bm, bn, kc, out_dtype, pre_silu=False, name="mm"):
    nb = a.shape[0] // bm
    return _gmm(a, [w[None] if w.ndim == 2 else w for w in ws], jnp.zeros((nb,), jnp.int32),
                jnp.full((1,), nb, jnp.int32), bm=bm, bn=bn, kc=kc, out_dtype=out_dtype,
                pre_silu=pre_silu, name=name)


def _norm_kernel(*refs, has_res, has_mod, n_part):
    it = iter(refs)
    x_ref = next(it)
    x = x_ref[...]
    if has_res:
        f_ref = next(it)
        gate_ref = next(it)
        f = f_ref[0]
        for p in range(1, n_part):
            f = f + f_ref[p]
        x = x + gate_ref[...] * f
    g_ref = next(it)
    if has_mod:
        sc_ref = next(it)
        sh_ref = next(it)
    if has_res:
        xo_ref = next(it)
        xo_ref[...] = x
    h_ref = next(it)
    y = x * lax.rsqrt(jnp.mean(x * x, axis=-1, keepdims=True) + RMS_EPS) * g_ref[...]
    if has_mod:
        y = y * (1.0 + sc_ref[...]) + sh_ref[...]
    h_ref[...] = y.astype(h_ref.dtype)


def _norm(x, g, *, f=None, gate=None, scale=None, shift=None, h_dtype=BF16, ts=256, name="norm"):
    b, s, d = x.shape
    ts = min(ts, s)
    assert s % ts == 0
    has_res, has_mod = f is not None, scale is not None
    row_spec = pl.BlockSpec((None, ts, d), lambda i, j: (i, j, 0))

    def mod_spec(v):
        if v.shape[1] == 1:
            return pl.BlockSpec((None, 1, d), lambda i, j: (i, 0, 0))
        return row_spec

    args, specs = [x], [row_spec]
    n_part = 0
    if has_res:
        n_part = f.shape[0]
        args += [f, gate]
        specs += [pl.BlockSpec((n_part, None, ts, d), lambda i, j: (0, i, j, 0)), mod_spec(gate)]
    args.append(g.reshape(1, d))
    specs.append(pl.BlockSpec((1, d), lambda i, j: (0, 0)))
    if has_mod:
        args += [scale, shift]
        specs += [mod_spec(scale), mod_spec(shift)]
    out_shape, out_specs = [], []
    if has_res:
        out_shape.append(jax.ShapeDtypeStruct((b, s, d), F32))
        out_specs.append(row_spec)
    out_shape.append(jax.ShapeDtypeStruct((b, s, d), h_dtype))
    out_specs.append(row_spec)
    outs = pl.pallas_call(
        functools.partial(_norm_kernel, has_res=has_res, has_mod=has_mod, n_part=n_part),
        grid=(b, s // ts), in_specs=specs, out_specs=out_specs, out_shape=out_shape,
        compiler_params=_params(2), name=name,
    )(*args)
    return (outs[0], outs[1]) if has_res else (None, outs[0])


def _neg_softplus(z):
    return -(jnp.maximum(z, 0.0) + jnp.log1p(jnp.exp(-jnp.abs(z))))


def _split_bf16(l):
    hi = l.astype(BF16)
    mid = (l - hi.astype(F32)).astype(BF16)
    return hi, mid


def _sb_prompt_block(q, kblk, vblk, bias, carry, acc, umat, mask):
    s = kblk.shape[0]
    z = lax.dot_general(q, kblk, (((1,), (1,)), ((), ())), preferred_element_type=F32)
    z = z * (HEAD_DIM_A ** -0.5) + bias
    l = _neg_softplus(z)
    if mask is not None:
        l = jnp.where(mask, l, 0.0)
    hi, mid = _split_bf16(l)
    bt = jnp.dot(jnp.concatenate([hi, mid], axis=1), umat, preferred_element_type=F32)
    a = jnp.exp(z + l + bt[:, :s] + carry)
    if mask is not None:
        a = jnp.where(mask, a, 0.0)
    acc = acc + jnp.dot(a.astype(BF16), vblk, preferred_element_type=F32)
    return carry + bt[:, s:], acc


def _sb_prompt_kernel(sb_ref, q_ref, k_ref, v_ref, umat_ref, o_ref, kb_ref, vb_ref, *, group, blk):
    kv = pl.program_id(1)
    i = pl.program_id(2)

    @pl.when(i == 0)
    def _convert_kv():
        kb_ref[...] = k_ref[...].astype(BF16)
        vb_ref[...] = v_ref[...].astype(BF16)

    umat = umat_ref[...]
    rows = lax.broadcasted_iota(jnp.int32, (blk, blk), 0)
    cols = lax.broadcasted_iota(jnp.int32, (blk, blk), 1)
    causal = cols < rows
    d = HEAD_DIM_A
    diag = pl.ds(pl.multiple_of(i * blk, blk), blk)
    for g in range(group):
        q = q_ref[:, g * d:(g + 1) * d].astype(BF16)
        bias = sb_ref[kv * group + g]
        zero = jnp.zeros((blk, d), F32)
        state = _sb_prompt_block(q, kb_ref[diag, :], vb_ref[diag, :], bias,
                                 jnp.zeros((blk, blk), F32), zero, umat, causal)

        def older(j, st):
            sl = pl.ds(pl.multiple_of((i - 1 - j) * blk, blk), blk)
            return _sb_prompt_block(q, kb_ref[sl, :], vb_ref[sl, :], bias, st[0], st[1], umat, None)

        _, acc = lax.fori_loop(0, i, older, state)
        o_ref[:, g * d:(g + 1) * d] = acc.astype(o_ref.dtype)


def _suffix_sum_matrix(blk, keys_on_lanes):
    j = np.arange(blk)
    strict = (j[:, None] > j[None, :]).astype(np.float32)
    if keys_on_lanes:
        half = np.concatenate([strict, np.ones((blk, blk), np.float32)], axis=1)
        return jnp.asarray(np.concatenate([half, half], axis=0), BF16)
    return jnp.asarray(np.concatenate([strict.T, strict.T], axis=1), BF16)


def _sb_prompt_attention(qkv, sb_bias, *, batch, seq, n_heads):
    d, blk = HEAD_DIM_A, PAGE_SIZE
    group = n_heads // N_KV_A
    nq = seq // blk
    q_cols = n_heads // group
    kernel = functools.partial(_sb_prompt_kernel, group=group, blk=blk)
    return pl.pallas_call(
        kernel,
        grid_spec=pltpu.PrefetchScalarGridSpec(
            num_scalar_prefetch=1, grid=(batch, N_KV_A, nq),
            in_specs=[
                pl.BlockSpec((blk, group * d), lambda b, kv, i, sb: (b * nq + i, kv)),
                pl.BlockSpec((seq, d), lambda b, kv, i, sb: (b, n_heads + kv)),
                pl.BlockSpec((seq, d), lambda b, kv, i, sb: (b, n_heads + N_KV_A + kv)),
                pl.BlockSpec((2 * blk, 2 * blk), lambda b, kv, i, sb: (0, 0)),
            ],
            out_specs=pl.BlockSpec((blk, group * d), lambda b, kv, i, sb: (b * nq + i, kv)),
            scratch_shapes=[pltpu.VMEM((seq, d), BF16), pltpu.VMEM((seq, d), BF16)]),
        out_shape=jax.ShapeDtypeStruct((batch * seq, n_heads * d), BF16),
        compiler_params=_params(3), name="sb_prompt",
    )(sb_bias, qkv, qkv, qkv, _suffix_sum_matrix(blk, True))


def _sb_sample_block(kpage, vpage, qbd, bias, ucat, carry, mask, n_kv, rows_per_kv):
    d = HEAD_DIM_A
    zt = jnp.dot(kpage, qbd, preferred_element_type=F32) * (d ** -0.5) + bias
    l = _neg_softplus(zt)
    if mask is not None:
        l = jnp.where(mask, l, 0.0)
    hi, mid = _split_bf16(l)
    between = jnp.dot(ucat, jnp.concatenate([hi, mid], axis=0), preferred_element_type=F32)
    a = jnp.exp(zt + l + between + carry)
    if mask is not None:
        a = jnp.where(mask, a, 0.0)
    at = a.T.astype(BF16)
    outs = [jnp.dot(at[kv * rows_per_kv:(kv + 1) * rows_per_kv, :], vpage[:, kv * d:(kv + 1) * d],
                    preferred_element_type=F32) for kv in range(n_kv)]
    return carry + jnp.sum(l, axis=0, keepdims=True), jnp.concatenate(outs, axis=0)


def _sb_sample_kernel(pt_ref, qbd_ref, bias_ref, knew_ref, vnew_ref, k_ref, v_ref, ucat_ref,
                      o_ref, carry_ref, acc_ref, *, n_kv, rows_per_kv, t_new):
    p = pl.program_id(1)
    blk = k_ref.shape[0]
    qbd = qbd_ref[...]
    bias = bias_ref[...]
    ucat = ucat_ref[...]
    block = functools.partial(_sb_sample_block, qbd=qbd, bias=bias, ucat=ucat, n_kv=n_kv,
                              rows_per_kv=rows_per_kv)

    @pl.when(p == 0)
    def _new_tokens():
        pad = jnp.zeros((blk - knew_ref.shape[0], knew_ref.shape[1]), F32)
        kn = jnp.concatenate([knew_ref[...], pad], axis=0).astype(BF16)
        vn = jnp.concatenate([vnew_ref[...], pad], axis=0).astype(BF16)
        s_idx = lax.broadcasted_iota(jnp.int32, (blk, qbd.shape[1]), 0)
        t_idx = lax.broadcasted_iota(jnp.int32, (blk, qbd.shape[1]), 1) % t_new
        carry, out = block(kn, vn, carry=jnp.zeros((1, qbd.shape[1]), F32), mask=s_idx < t_idx)
        carry_ref[...] = jnp.broadcast_to(carry, carry_ref.shape)
        acc_ref[...] = out

    carry, out = block(k_ref[...].astype(BF16), v_ref[...].astype(BF16), carry=carry_ref[0:1, :],
                       mask=None)
    carry_ref[...] = jnp.broadcast_to(carry, carry_ref.shape)
    acc_ref[...] += out

    @pl.when(p == pl.num_programs(1) - 1)
    def _finish():
        o_ref[...] = acc_ref[...]


def _sb_sample_attention(q, k_new, v_new, k_pool, v_pool, page_table, sb_bias):
    b, t, _ = q.shape
    d, n_kv = HEAD_DIM_A, N_KV_A
    n_heads = q.shape[-1] // d
    group = n_heads // n_kv
    rows_per_kv = group * t
    c = n_kv * rows_per_kv
    n_pages = page_table.shape[1]
    q5 = q.reshape(b, t, n_kv, group, d).transpose(0, 2, 4, 3, 1).reshape(b, n_kv, d, rows_per_kv)
    qbd = jnp.einsum("bkdc,kj->bkdjc", q5, jnp.eye(n_kv, dtype=q.dtype))
    qbd = qbd.reshape(b, n_kv * d, c).astype(BF16)
    bias = jnp.repeat(sb_bias.astype(F32), t).reshape(1, c)
    pad_t = 8
    knew = jnp.zeros((b, pad_t, n_kv * d), F32).at[:, :t].set(k_new)
    vnew = jnp.zeros((b, pad_t, n_kv * d), F32).at[:, :t].set(v_new)
    kernel = functools.partial(_sb_sample_kernel, n_kv=n_kv, rows_per_kv=rows_per_kv, t_new=t)
    page = lambda i, p, pt: (pt[i, n_pages - 1 - p], 0, 0)
    per_b = lambda i, p, pt: (i, 0, 0)
    o = pl.pallas_call(
        kernel,
        grid_spec=pltpu.PrefetchScalarGridSpec(
            num_scalar_prefetch=1, grid=(b, n_pages),
            in_specs=[
                pl.BlockSpec((None, n_kv * d, c), per_b),
                pl.BlockSpec((1, c), lambda i, p, pt: (0, 0)),
                pl.BlockSpec((None, pad_t, n_kv * d), per_b),
                pl.BlockSpec((None, pad_t, n_kv * d), per_b),
                pl.BlockSpec((None, PAGE_SIZE, n_kv * d), page),
                pl.BlockSpec((None, PAGE_SIZE, n_kv * d), page),
                pl.BlockSpec((PAGE_SIZE, 2 * PAGE_SIZE), lambda i, p, pt: (0, 0)),
            ],
            out_specs=pl.BlockSpec((None, c, d), per_b),
            scratch_shapes=[pltpu.VMEM((8, c), F32), pltpu.VMEM((c, d), F32)]),
        out_shape=jax.ShapeDtypeStruct((b, c, d), F32),
        compiler_params=_params(2), name="sb_sample",
    )(page_table, qbd, bias, knew, vnew, k_pool, v_pool, _suffix_sum_matrix(PAGE_SIZE, False))
    return o.reshape(b, n_kv, group, t, d).transpose(0, 3, 1, 2, 4).reshape(b, t, n_heads * d)


def _t5_bucket_table(n_q, n_k, offset):
    dist = offset + np.arange(n_q)[:, None] - np.arange(n_k)[None, :]
    valid = (dist >= 0) & (dist < WINDOW)
    dd = np.maximum(dist, 0)
    max_exact = N_BUCKETS // 2
    far = max_exact + (np.log(np.maximum(dd, 1).astype(np.float32) / np.float32(max_exact))
                       / np.float32(math.log(MAX_DISTANCE / max_exact))
                       * np.float32(N_BUCKETS - max_exact)).astype(np.int32)
    bucket = np.where(dd < max_exact, dd, np.minimum(far, N_BUCKETS - 1))
    return bucket.astype(np.int32), valid


def _sink_softmax_pv(z, sink, v):
    m = jnp.maximum(jnp.max(z, axis=-1, keepdims=True), sink)
    e = jnp.exp(z - m)
    p = e / (jnp.sum(e, axis=-1, keepdims=True) + jnp.exp(sink - m))
    return jnp.dot(p.astype(BF16), v, preferred_element_type=F32)


def _swa_prompt_kernel(sink_ref, q_ref, kprev_ref, kcur_ref, vprev_ref, vcur_ref, bias_ref,
                       valid_ref, o_ref, *, heads_per_step, group):
    n = pl.program_id(1)
    hp = pl.program_id(2)
    d = HEAD_DIM_B
    w = kcur_ref.shape[0]
    col = lax.broadcasted_iota(jnp.int32, (w, 2 * w), 1)
    mask = jnp.logical_and(valid_ref[...] > 0.5, col >= jnp.where(n > 0, 0, w))
    kcat = jnp.concatenate([kprev_ref[...], kcur_ref[...]], axis=0).astype(BF16)
    vcat = jnp.concatenate([vprev_ref[...], vcur_ref[...]], axis=0).astype(BF16)
    for kvl in range(heads_per_step // group):
        k = kcat[:, kvl * d:(kvl + 1) * d]
        v = vcat[:, kvl * d:(kvl + 1) * d]
        for g in range(group):
            hl = kvl * group + g
            q = q_ref[:, hl * d:(hl + 1) * d].astype(BF16)
            z = lax.dot_general(q, k, (((1,), (1,)), ((), ())), preferred_element_type=F32)
            z = z * (d ** -0.5) + bias_ref[hl]
            z = jnp.where(mask, z, -jnp.inf)
            sink = sink_ref[hp * heads_per_step + hl]
            o_ref[:, hl * d:(hl + 1) * d] = _sink_softmax_pv(z, sink, v).astype(o_ref.dtype)


def _swa_prompt_attention(qkv, sinks, bias, valid, *, batch, seq, n_heads):
    d, w = HEAD_DIM_B, WINDOW
    group = n_heads // N_KV_B
    kv_per_step = LANES // d
    hps = kv_per_step * group
    nb = seq // w
    n_steps = n_heads // hps
    q_blocks = n_heads * d // LANES
    k_blocks = N_KV_B * d // LANES
    cur = lambda off: (lambda b, n, hp, s: (b * nb + n, off + hp))
    prev = lambda off: (lambda b, n, hp, s: (b * nb + jnp.maximum(n - 1, 0), off + hp))
    kernel = functools.partial(_swa_prompt_kernel, heads_per_step=hps, group=group)
    return pl.pallas_call(
        kernel,
        grid_spec=pltpu.PrefetchScalarGridSpec(
            num_scalar_prefetch=1, grid=(batch, nb, n_steps),
            in_specs=[
                pl.BlockSpec((w, hps * d), lambda b, n, hp, s: (b * nb + n, hp)),
                pl.BlockSpec((w, LANES), prev(q_blocks)),
                pl.BlockSpec((w, LANES), cur(q_blocks)),
                pl.BlockSpec((w, LANES), prev(q_blocks + k_blocks)),
                pl.BlockSpec((w, LANES), cur(q_blocks + k_blocks)),
                pl.BlockSpec((hps, w, 2 * w), lambda b, n, hp, s: (hp, 0, 0)),
                pl.BlockSpec((w, 2 * w), lambda b, n, hp, s: (0, 0)),
            ],
            out_specs=pl.BlockSpec((w, hps * d), lambda b, n, hp, s: (b * nb + n, hp))),
        out_shape=jax.ShapeDtypeStruct((batch * seq, n_heads * d), BF16),
        compiler_params=_params(3), name="swa_prompt",
    )(sinks, qkv, qkv, qkv, qkv, qkv, bias, valid)


def _swa_sample_kernel(q_ref, kbuf_ref, vbuf_ref, knew_ref, vnew_ref, bias_ref, valid_ref, sink_ref,
                       o_ref, *, n_kv):
    d = HEAD_DIM_B
    kcat = jnp.concatenate([kbuf_ref[...], knew_ref[...]], axis=0).astype(BF16)
    vcat = jnp.concatenate([vbuf_ref[...], vnew_ref[...]], axis=0).astype(BF16)
    mask = valid_ref[...] > 0.5
    for kv in range(n_kv):
        q = q_ref[kv].astype(BF16)
        k = kcat[:, kv * d:(kv + 1) * d]
        v = vcat[:, kv * d:(kv + 1) * d]
        z = lax.dot_general(q, k, (((1,), (1,)), ((), ())), preferred_element_type=F32)
        z = z * (d ** -0.5) + bias_ref[kv]
        z = jnp.where(mask, z, -jnp.inf)
        o_ref[kv] = _sink_softmax_pv(z, sink_ref[kv], v)


def _swa_sample_attention(q, k_new, v_new, k_buf, v_buf, sinks, rel_bias):
    b, t, _ = q.shape
    d, n_kv = HEAD_DIM_B, N_KV_B
    n_heads = q.shape[-1] // d
    group = n_heads // n_kv
    n_buf = k_buf.shape[1]
    pad_t = LANES
    rows = group * t
    bucket, valid = _t5_bucket_table(t, n_buf + pad_t, n_buf)
    valid[:, n_buf + t:] = False
    bias = rel_bias[jnp.asarray(bucket)].astype(F32)
    bias = bias.transpose(2, 0, 1).reshape(n_kv, rows, n_buf + pad_t)
    valid_rows = jnp.asarray(np.tile(valid, (group, 1)).astype(np.float32))
    qr = q.reshape(b, t, n_kv, group, d).transpose(0, 2, 3, 1, 4).reshape(b, n_kv, rows, d)
    knew = jnp.zeros((b, pad_t, n_kv * d), F32).at[:, :t].set(k_new)
    vnew = jnp.zeros((b, pad_t, n_kv * d), F32).at[:, :t].set(v_new)
    sink_rows = jnp.repeat(sinks.astype(F32), t).reshape(n_kv, rows, 1)
    per_b3 = lambda i: (i, 0, 0)
    const3 = lambda i: (0, 0, 0)
    o = pl.pallas_call(
        functools.partial(_swa_sample_kernel, n_kv=n_kv),
        grid=(b,),
        in_specs=[
            pl.BlockSpec((None, n_kv, rows, d), lambda i: (i, 0, 0, 0)),
            pl.BlockSpec((None, n_buf, n_kv * d), per_b3),
            pl.BlockSpec((None, n_buf, n_kv * d), per_b3),
            pl.BlockSpec((None, pad_t, n_kv * d), per_b3),
            pl.BlockSpec((None, pad_t, n_kv * d), per_b3),
            pl.BlockSpec((n_kv, rows, n_buf + pad_t), const3),
            pl.BlockSpec((rows, n_buf + pad_t), lambda i: (0, 0)),
            pl.BlockSpec((n_kv, rows, 1), const3),
        ],
        out_specs=pl.BlockSpec((None, n_kv, rows, d), lambda i: (i, 0, 0, 0)),
        out_shape=jax.ShapeDtypeStruct((b, n_kv, rows, d), F32),
        compiler_params=_params(1), name="swa_sample",
    )(qr, k_buf, v_buf, knew, vnew, bias, valid_rows, sink_rows)
    return o.reshape(b, n_kv, group, t, d).transpose(0, 3, 1, 2, 4).reshape(b, t, n_heads * d)


def _router_kernel(h_ref, w_ref, b_ref, idx_ref, gate_ref, *, n_experts):
    logits = jnp.dot(h_ref[...], w_ref[...].astype(BF16), preferred_element_type=F32) + b_ref[...]
    lane_i = lax.broadcasted_iota(jnp.int32, logits.shape, 1)
    lane = lane_i.astype(F32)
    neg = -jnp.inf
    lg = jnp.where(lane_i < n_experts, logits, neg)
    m1 = jnp.max(lg, axis=-1, keepdims=True)
    i1 = jnp.min(jnp.where(lg == m1, lane, float(LANES)), axis=-1, keepdims=True)
    lg2 = jnp.where(lane == i1, neg, lg)
    m2 = jnp.max(lg2, axis=-1, keepdims=True)
    i2 = jnp.min(jnp.where(lg2 == m2, lane, float(LANES)), axis=-1, keepdims=True)
    e2 = jnp.exp(m2 - m1)
    den = 1.0 + e2
    idx_ref[...] = jnp.where(lane_i == 0, i1, jnp.where(lane_i == 1, i2, 0.0)).astype(jnp.int32)
    gate_ref[...] = jnp.where(lane_i == 0, 1.0 / den, jnp.where(lane_i == 1, e2 / den, 0.0))


def _router(h, w_router, b_router, *, bm):
    m, k = h.shape
    n_experts = w_router.shape[1]
    w = jnp.zeros((k, LANES), F32).at[:, :n_experts].set(w_router)
    bvec = jnp.zeros((1, LANES), F32).at[0, :n_experts].set(b_router.astype(F32))
    idx, gate = pl.pallas_call(
        functools.partial(_router_kernel, n_experts=n_experts),
        grid=(m // bm,),
        in_specs=[pl.BlockSpec((bm, k), lambda i: (i, 0)),
                  pl.BlockSpec((k, LANES), lambda i: (0, 0)),
                  pl.BlockSpec((1, LANES), lambda i: (0, 0))],
        out_specs=[pl.BlockSpec((bm, LANES), lambda i: (i, 0))] * 2,
        out_shape=[jax.ShapeDtypeStruct((m, LANES), jnp.int32),
                   jax.ShapeDtypeStruct((m, LANES), F32)],
        compiler_params=_params(1), name="router",
    )(h, w, bvec)
    return idx[:, :TOP_K], gate[:, :TOP_K]


def _moe(h, w_router, b_router, w_gate_e, w_up_e, w_down_e, *, bm_route, bm, bn_up, bn_down, kc_down):
    n, d = h.shape
    n_experts = w_gate_e.shape[0]
    idx, gate = _router(h, w_router, b_router, bm=bm_route)
    flat_e = idx.reshape(-1)
    onehot = (flat_e[:, None] == jnp.arange(n_experts, dtype=jnp.int32)[None, :]).astype(jnp.int32)
    csum = jnp.cumsum(onehot, axis=0)
    rank = jnp.take_along_axis(csum, flat_e[:, None], axis=1)[:, 0] - 1
    counts = csum[-1]
    blocks_e = (counts + bm - 1) // bm
    blocks_end = jnp.cumsum(blocks_e)
    dest = (blocks_end - blocks_e)[flat_e] * bm + rank
    n_blocks = (n * TOP_K) // bm + n_experts
    n_used = blocks_end[-1:].astype(jnp.int32)
    blk = jnp.minimum(jnp.arange(n_blocks, dtype=jnp.int32), n_used[0] - 1)
    blk_e = jnp.minimum(jnp.searchsorted(blocks_end, blk, side="right"), n_experts - 1).astype(jnp.int32)
    slot_tok = jnp.zeros((n_blocks * bm,), jnp.int32).at[dest].set(
        jnp.repeat(jnp.arange(n, dtype=jnp.int32), TOP_K))
    xg = h[slot_tok]
    act = _gmm(xg, [w_gate_e, w_up_e], blk_e, n_used, bm=bm, bn=bn_up, kc=d, out_dtype=BF16,
               name="moe_up")[0]
    yb = _gmm(act, [w_down_e], blk_e, n_used, bm=bm, bn=bn_down, kc=kc_down, out_dtype=F32,
              name="moe_down")
    dest2 = dest.reshape(n, TOP_K)
    y = yb[:, dest2[:, 0]] * gate[None, :, 0:1] + yb[:, dest2[:, 1]] * gate[None, :, 1:2]
    return y


def kernel(x_prompt, x_sample, c_prompt, c_sample, cache_k_a, cache_v_a, page_table, cache_k_b, cache_v_b, g_mix, g_ffn, g_final, w_ada, b_ada, w_qkv_a, w_o_a, sb_bias_a, w_qkv_b, w_o_b, sinks_b, rel_bias, w_gate_d, w_up_d, w_down_d, w_router, b_router, w_gate_e, w_up_e, w_down_e):
    bp, seq, d = x_prompt.shape
    bs, t, _ = x_sample.shape
    n_p, n_s = bp * seq, bs * t
    n = n_p + n_s
    n_heads_a = d // HEAD_DIM_A
    n_heads_b = d // HEAD_DIM_B
    depth = w_ada.shape[0]
    bm = 832 if n % 832 == 0 else 128
    assert n % bm == 0 and n_p % PAGE_SIZE == 0

    c_rows = 64
    c_all = jnp.zeros((c_rows, d), F32).at[:bp + bs].set(jnp.concatenate([c_prompt, c_sample], axis=0))
    ada = _gmm(jnp.concatenate([c_all] * depth, axis=0), [w_ada], jnp.arange(depth, dtype=jnp.int32),
               jnp.full((1,), depth, jnp.int32), bm=c_rows, bn=512, kc=d, out_dtype=F32,
               pre_silu=True, name="adaln")[0]
    ada = ada.reshape(depth, c_rows, 6, d)[:, :bp + bs] + b_ada.reshape(depth, 1, 6, d)

    def mods(i, which):
        m = ada[i, :, which]
        return m[:bp].reshape(bp, 1, d), jnp.repeat(m[bp:], t, axis=0).reshape(1, n_s, d)

    def norm_both(xp, xs, g, i, which_scale, which_shift, f=None, gate_of=None):
        sc_p, sc_s = mods(i, which_scale)
        sh_p, sh_s = mods(i, which_shift)
        if f is None:
            _, hp = _norm(xp, g, scale=sc_p, shift=sh_p)
            _, hs = _norm(xs, g, scale=sc_s, shift=sh_s)
        else:
            gt_p, gt_s = mods(*gate_of)
            fp = f[:, :n_p].reshape(-1, bp, seq, d)
            fs = f[:, n_p:].reshape(-1, 1, n_s, d)
            xp, hp = _norm(xp, g, f=fp, gate=gt_p, scale=sc_p, shift=sh_p)
            xs, hs = _norm(xs, g, f=fs, gate=gt_s, scale=sc_s, shift=sh_s)
        return xp, xs, jnp.concatenate([hp.reshape(n_p, d), hs.reshape(n_s, d)], axis=0)

    xp = x_prompt
    xs = x_sample.reshape(1, n_s, d)

    xp, xs, h = norm_both(xp, xs, g_mix[0], 0, 1, 0)
    qkv_a = _dense_mm(h, [w_qkv_a[0]], bm=bm, bn=512, kc=d, out_dtype=F32, name="qkv_a")[0]
    kv_w = N_KV_A * HEAD_DIM_A
    o_p = _sb_prompt_attention(qkv_a, sb_bias_a[0].astype(F32), batch=bp, seq=seq, n_heads=n_heads_a)
    qkv_s = qkv_a[n_p:].reshape(bs, t, -1)
    k_pool = cache_k_a[0].reshape(cache_k_a.shape[1], PAGE_SIZE, kv_w)
    v_pool = cache_v_a[0].reshape(cache_v_a.shape[1], PAGE_SIZE, kv_w)
    o_s = _sb_sample_attention(qkv_s[..., :d], qkv_s[..., d:d + kv_w], qkv_s[..., d + kv_w:],
                               k_pool, v_pool, page_table, sb_bias_a[0])
    o = jnp.concatenate([o_p, o_s.reshape(n_s, d).astype(BF16)], axis=0)
    f = _dense_mm(o, [w_o_a[0]], bm=bm, bn=512, kc=d, out_dtype=F32, name="wo_a")
    xp, xs, h = norm_both(xp, xs, g_ffn[0], 0, 4, 3, f=f, gate_of=(0, 2))
    act = _dense_mm(h, [w_gate_d[0], w_up_d[0]], bm=bm // 2, bn=512, kc=d, out_dtype=BF16, name="ffn_up")[0]
    f = _dense_mm(act, [w_down_d[0]], bm=bm // 2, bn=512, kc=w_down_d.shape[1] // 2, out_dtype=F32,
                  name="ffn_down")

    xp, xs, h = norm_both(xp, xs, g_mix[1], 1, 1, 0, f=f, gate_of=(0, 5))
    qkv_b = _dense_mm(h, [w_qkv_b[0]], bm=bm, bn=512, kc=d, out_dtype=F32, name="qkv_b")[0]
    kvb_w = N_KV_B * HEAD_DIM_B
    bucket, valid = _t5_bucket_table(WINDOW, 2 * WINDOW, WINDOW)
    band_bias = rel_bias[jnp.asarray(bucket)].astype(F32).transpose(2, 0, 1)
    o_p = _swa_prompt_attention(qkv_b, sinks_b[0].astype(F32), band_bias,
                                jnp.asarray(valid.astype(np.float32)), batch=bp, seq=seq,
                                n_heads=n_heads_b)
    qkvb_s = qkv_b[n_p:].reshape(bs, t, -1)
    n_buf = cache_k_b.shape[2]
    kb_new, vb_new = qkvb_s[..., d:d + kvb_w], qkvb_s[..., d + kvb_w:]
    o_s = _swa_sample_attention(qkvb_s[..., :d], kb_new, vb_new,
                                cache_k_b[0].reshape(bs, n_buf, kvb_w),
                                cache_v_b[0].reshape(bs, n_buf, kvb_w), sinks_b[0], rel_bias)
    o = jnp.concatenate([o_p, o_s.reshape(n_s, d).astype(BF16)], axis=0)
    f = _dense_mm(o, [w_o_b[0]], bm=bm, bn=512, kc=d, out_dtype=F32, name="wo_b")
    xp, xs, h = norm_both(xp, xs, g_ffn[1], 1, 4, 3, f=f, gate_of=(1, 2))
    f = _moe(h, w_router[0], b_router[0], w_gate_e[0], w_up_e[0], w_down_e[0], bm_route=bm,
             bm=256, bn_up=512, bn_down=512, kc_down=w_down_e.shape[2] // 2)

    gt_p, gt_s = mods(1, 5)
    _, y_p = _norm(xp, g_final, f=f[:, :n_p].reshape(-1, bp, seq, d), gate=gt_p, h_dtype=F32)
    _, y_s = _norm(xs, g_final, f=f[:, n_p:].reshape(-1, 1, n_s, d), gate=gt_s, h_dtype=F32)

    qa_p = qkv_a[:n_p]
    k_a_prompt = qa_p[:, d:d + kv_w].reshape(1, bp, seq // PAGE_SIZE, PAGE_SIZE, N_KV_A, HEAD_DIM_A)
    v_a_prompt = qa_p[:, d + kv_w:].reshape(1, bp, seq // PAGE_SIZE, PAGE_SIZE, N_KV_A, HEAD_DIM_A)
    k_a_sample = qkv_s[..., d:d + kv_w].reshape(1, bs, t, N_KV_A, HEAD_DIM_A)
    v_a_sample = qkv_s[..., d + kv_w:].reshape(1, bs, t, N_KV_A, HEAD_DIM_A)
    n_keep = min(WINDOW, seq)
    qb_p = qkv_b[:n_p].reshape(bp, seq, -1)[:, seq - n_keep:]
    k_b_prompt = qb_p[..., d:d + kvb_w].reshape(1, bp, n_keep, N_KV_B, HEAD_DIM_B)
    v_b_prompt = qb_p[..., d + kvb_w:].reshape(1, bp, n_keep, N_KV_B, HEAD_DIM_B)
    k_b_sample = jnp.concatenate([cache_k_b[0][:, t:], kb_new.reshape(bs, t, N_KV_B, HEAD_DIM_B)], axis=1)[None]
    v_b_sample = jnp.concatenate([cache_v_b[0][:, t:], vb_new.reshape(bs, t, N_KV_B, HEAD_DIM_B)], axis=1)[None]
    return (y_p, y_s.reshape(bs, t, d), k_a_prompt, v_a_prompt, k_a_sample, v_a_sample,
            k_b_prompt, v_b_prompt, k_b_sample, v_b_sample)
```

```python
import functools
import math

import numpy as np
import jax
import jax.numpy as jnp
from jax import lax
from jax.experimental import pallas as pl
from jax.experimental.pallas import tpu as pltpu

F32 = jnp.float32
BF16 = jnp.bfloat16

HEAD_DIM_A = 128
N_KV_A = 8
HEAD_DIM_B = 64
N_KV_B = 8
PAGE_SIZE = 128
WINDOW = 128
N_BUCKETS = 32
MAX_DISTANCE = 128
TOP_K = 2
RMS_EPS = 1e-5

LANES = 128
V7X_VMEM_BYTES = 64 * 1024 * 1024
VMEM_LIMIT = V7X_VMEM_BYTES - 8 * 1024 * 1024
CAST_ROWS = 256


def _params(n_axes):
    return pltpu.CompilerParams(dimension_semantics=("arbitrary",) * n_axes,
                                vmem_limit_bytes=VMEM_LIMIT)


def _sigmoid(x):
    return 1.0 / (1.0 + jnp.exp(-x))


def _gmm_kernel(be_ref, nu_ref, a_ref, *refs, n_w, pre_silu, has_acc):
    w_refs = refs[:n_w]
    acc_ref = refs[n_w] if has_acc else None
    o_ref = refs[n_w + has_acc]
    wb_refs = refs[n_w + has_acc + 1:]
    r = pl.program_id(1)
    e = be_ref[r]
    prev = be_ref[jnp.maximum(r - 1, 0)]

    @pl.when(jnp.logical_or(r == 0, e != prev))
    def _convert_weights():
        kc = wb_refs[0].shape[0]

        def body(i, c):
            rows = pl.ds(pl.multiple_of(i * CAST_ROWS, CAST_ROWS), CAST_ROWS)
            for w_ref, wb_ref in zip(w_refs, wb_refs):
                wb_ref[rows, :] = w_ref[0, rows, :].astype(BF16)
            return c

        lax.fori_loop(0, kc // CAST_ROWS, body, 0)

    @pl.when(r < nu_ref[0])
    def _compute():
        a = a_ref[...]
        if pre_silu:
            a = a.astype(F32)
            a = a * _sigmoid(a)
        a = a.astype(BF16)
        if n_w == 2:
            g = jnp.dot(a, wb_refs[0][...], preferred_element_type=F32)
            u = jnp.dot(a, wb_refs[1][...], preferred_element_type=F32)
            res = g * _sigmoid(g) * u
        else:
            res = jnp.dot(a, wb_refs[0][...], preferred_element_type=F32)
        if has_acc:
            res = acc_ref[...] + res
        o_ref[...] = res.astype(o_ref.dtype)

    @pl.when(r >= nu_ref[0])
    def _unused_block():
        o_ref[...] = acc_ref[...] if has_acc else jnp.zeros_like(o_ref)


def _gmm(a, ws, blk_e, n_used, *, bm, bn, kc, out_dtype, pre_silu=False, name="gmm"):
    m, k = a.shape
    _, _, n = ws[0].shape
    assert m % bm == 0 and n % bn == 0 and k % kc == 0 and kc % CAST_ROWS == 0
    n_w = len(ws)
    out = None
    for kh in range(k // kc):
        has_acc = out is not None
        in_specs = [pl.BlockSpec((bm, kc), lambda j, r, be, nu, kh=kh: (r, kh))]
        in_specs += [pl.BlockSpec((1, kc, bn), lambda j, r, be, nu, kh=kh: (be[r], kh, j))
                     for _ in ws]
        io_spec = pl.BlockSpec((bm, bn), lambda j, r, be, nu: (r, j))
        args = [blk_e, n_used, a, *ws]
        if has_acc:
            in_specs.append(io_spec)
            args.append(out)
        out = pl.pallas_call(
            functools.partial(_gmm_kernel, n_w=n_w, pre_silu=pre_silu, has_acc=has_acc),
            grid_spec=pltpu.PrefetchScalarGridSpec(
                num_scalar_prefetch=2, grid=(n // bn, m // bm), in_specs=in_specs,
                out_specs=io_spec, scratch_shapes=[pltpu.VMEM((kc, bn), BF16) for _ in ws]),
            out_shape=jax.ShapeDtypeStruct((m, n), out_dtype),
            input_output_aliases={3 + n_w: 0} if has_acc else {},
            compiler_params=_params(2), name=name,
        )(*args)
    return out


def _dense_mm(a, ws, *, bm, bn, kc, out_dtype, pre_silu=False, name="mm"):
    nb = a.shape[0] // bm
    return _gmm(a, [w[None] for w in ws], jnp.zeros((nb,), jnp.int32),
                jnp.full((1,), nb, jnp.int32), bm=bm, bn=bn, kc=kc, out_dtype=out_dtype,
                pre_silu=pre_silu, name=name)


def _norm_kernel(*refs, has_res, has_mod, want_x):
    it = iter(refs)
    x = next(it)[...]
    if has_res:
        f_ref = next(it)
        gate_ref = next(it)
        x = x + gate_ref[...] * f_ref[...]
    g_ref = next(it)
    if has_mod:
        sc_ref = next(it)
        sh_ref = next(it)
    if want_x:
        next(it)[...] = x
    h_ref = next(it)
    y = x * lax.rsqrt(jnp.mean(x * x, axis=-1, keepdims=True) + RMS_EPS) * g_ref[...]
    if has_mod:
        y = y * (1.0 + sc_ref[...]) + sh_ref[...]
    h_ref[...] = y.astype(h_ref.dtype)


def _norm(x, g, *, row0, batch, rows, out_rows, want_x=False, f=None, gate=None, scale=None,
          shift=None, h_dtype=BF16, ts=LANES, name="norm"):
    n, d = x.shape
    ts = min(ts, rows)
    assert rows % ts == 0 and row0 % ts == 0 and n % ts == 0
    nt = rows // ts
    has_res, has_mod = f is not None, scale is not None
    assert has_res or not want_x
    in_blk0 = row0 // ts
    full = out_rows == n
    out_blk0 = in_blk0 if full else 0
    n_blocks = (n - row0) // ts if full else batch * nt
    in_spec = pl.BlockSpec((ts, d), lambda i: (in_blk0 + i, 0))
    out_spec = pl.BlockSpec((ts, d), lambda i: (out_blk0 + i, 0))

    def mod_spec(v):
        if v.shape[1] == 1:
            return pl.BlockSpec((None, 1, d), lambda i: (jnp.minimum(i // nt, batch - 1), 0, 0))
        assert batch == 1
        return pl.BlockSpec((None, ts, d), lambda i: (0, i, 0))

    args, specs = [x], [in_spec]
    if has_res:
        args += [f, gate]
        specs += [in_spec, mod_spec(gate)]
    args.append(g.reshape(1, d))
    specs.append(pl.BlockSpec((1, d), lambda i: (0, 0)))
    if has_mod:
        args += [scale, shift]
        specs += [mod_spec(scale), mod_spec(shift)]
    out_shape, out_specs = [], []
    if want_x:
        out_shape.append(jax.ShapeDtypeStruct((out_rows, d), F32))
        out_specs.append(out_spec)
    out_shape.append(jax.ShapeDtypeStruct((out_rows, d), h_dtype))
    out_specs.append(out_spec)
    outs = pl.pallas_call(
        functools.partial(_norm_kernel, has_res=has_res, has_mod=has_mod, want_x=want_x),
        grid=(n_blocks,), in_specs=specs, out_specs=out_specs, out_shape=out_shape,
        compiler_params=_params(1), name=name,
    )(*args)
    return (outs[0], outs[1]) if want_x else (None, outs[0])


def _neg_softplus(z):
    return -(jnp.maximum(z, 0.0) + jnp.log(1.0 + jnp.exp(-jnp.abs(z))))


def _split_bf16(l):
    hi = l.astype(BF16)
    mid = (l - hi.astype(F32)).astype(BF16)
    return hi, mid


def _suffix_sum_matrix(blk, keys_on_lanes):
    j = np.arange(blk)
    strict = (j[:, None] > j[None, :]).astype(np.float32)
    if keys_on_lanes:
        half = np.concatenate([strict, np.ones((blk, blk), np.float32)], axis=1)
        return jnp.asarray(np.concatenate([half, half], axis=0), BF16)
    return jnp.asarray(np.concatenate([strict.T, strict.T], axis=1), BF16)


def _sb_prompt_kernel(sb_ref, q_ref, k_ref, v_ref, umat_ref, o_ref, kb_ref, vb_ref, qb_ref,
                      bias_ref, carry_ref, acc_ref, *, group, blk):
    kv = pl.program_id(1)
    i = pl.program_id(2)
    d = HEAD_DIM_A

    @pl.when(i == 0)
    def _per_kv_head_setup():
        kb_ref[...] = k_ref[...].astype(BF16)
        vb_ref[...] = v_ref[...].astype(BF16)
        for g in range(group):
            bias_ref[g * blk:(g + 1) * blk, :] = jnp.full((blk, 2 * blk), sb_ref[kv * group + g], F32)

    for g in range(group):
        qb_ref[g * blk:(g + 1) * blk, :] = q_ref[:, g * d:(g + 1) * d].astype(BF16)
    rows = lax.broadcasted_iota(jnp.int32, (group * blk, blk), 0)
    cols = lax.broadcasted_iota(jnp.int32, (group * blk, blk), 1)
    causal = cols < (rows & (blk - 1))
    nt_dims = (((1,), (1,)), ((), ()))

    def block(off, mask, first):
        sl = pl.ds(pl.multiple_of(off, blk), blk)
        z = lax.dot_general(qb_ref[...], kb_ref[sl, :], nt_dims, preferred_element_type=F32)
        z = z * (d ** -0.5) + bias_ref[:, :blk]
        l = _neg_softplus(z)
        if mask is not None:
            l = jnp.where(mask, l, 0.0)
        hi, mid = _split_bf16(l)
        bt = jnp.dot(jnp.concatenate([hi, mid], axis=1), umat_ref[...], preferred_element_type=F32)
        logit = z + l + bt[:, :blk]
        if not first:
            logit = logit + carry_ref[...]
        a = jnp.exp(logit)
        if mask is not None:
            a = jnp.where(mask, a, 0.0)
        pv = jnp.dot(a.astype(BF16), vb_ref[sl, :], preferred_element_type=F32)
        if first:
            acc_ref[...] = pv
            carry_ref[...] = bt[:, blk:]
        else:
            acc_ref[...] += pv
            carry_ref[...] += bt[:, blk:]

    def block_pair(off):
        sl = pl.ds(pl.multiple_of(off, blk), 2 * blk)
        z = lax.dot_general(qb_ref[...], kb_ref[sl, :], nt_dims, preferred_element_type=F32)
        z = z * (d ** -0.5) + bias_ref[...]
        l = _neg_softplus(z)
        hi, mid = _split_bf16(l)
        umat = umat_ref[...]
        bt_old = jnp.dot(jnp.concatenate([hi[:, :blk], mid[:, :blk]], axis=1), umat,
                         preferred_element_type=F32)
        bt_new = jnp.dot(jnp.concatenate([hi[:, blk:], mid[:, blk:]], axis=1), umat,
                         preferred_element_type=F32)
        carry = carry_ref[...]
        newer = jnp.concatenate([bt_old[:, :blk] + bt_new[:, blk:] + carry, bt_new[:, :blk] + carry],
                                axis=1)
        a = jnp.exp(z + l + newer)
        acc_ref[...] += jnp.dot(a.astype(BF16), vb_ref[sl, :], preferred_element_type=F32)
        carry_ref[...] = carry + bt_new[:, blk:] + bt_old[:, blk:]

    block(i * blk, causal, True)

    def older_pair(j, c):
        block_pair((i - 2 - 2 * j) * blk)
        return c

    lax.fori_loop(0, lax.shift_right_logical(i, 1), older_pair, 0)

    @pl.when((i & 1) == 1)
    def _oldest_block():
        block(0, None, False)

    for g in range(group):
        o_ref[:, g * d:(g + 1) * d] = acc_ref[g * blk:(g + 1) * blk, :].astype(o_ref.dtype)


def _sb_prompt_attention(qkv, sb_bias, *, batch, seq, n_heads):
    d, blk = HEAD_DIM_A, PAGE_SIZE
    assert blk & (blk - 1) == 0
    group = n_heads // N_KV_A
    nq = seq // blk
    kernel = functools.partial(_sb_prompt_kernel, group=group, blk=blk)
    return pl.pallas_call(
        kernel,
        grid_spec=pltpu.PrefetchScalarGridSpec(
            num_scalar_prefetch=1, grid=(batch, N_KV_A, nq),
            in_specs=[
                pl.BlockSpec((blk, group * d), lambda b, kv, i, sb: (b * nq + i, kv)),
                pl.BlockSpec((seq, d), lambda b, kv, i, sb: (b, n_heads + kv)),
                pl.BlockSpec((seq, d), lambda b, kv, i, sb: (b, n_heads + N_KV_A + kv)),
                pl.BlockSpec((2 * blk, 2 * blk), lambda b, kv, i, sb: (0, 0)),
            ],
            out_specs=pl.BlockSpec((blk, group * d), lambda b, kv, i, sb: (b * nq + i, kv)),
            scratch_shapes=[pltpu.VMEM((seq, d), BF16), pltpu.VMEM((seq, d), BF16),
                            pltpu.VMEM((group * blk, d), BF16), pltpu.VMEM((group * blk, 2 * blk), F32),
                            pltpu.VMEM((group * blk, blk), F32), pltpu.VMEM((group * blk, d), F32)]),
        out_shape=jax.ShapeDtypeStruct((batch * seq, n_heads * d), BF16),
        compiler_params=_params(3), name="sb_prompt",
    )(sb_bias, qkv, qkv, qkv, _suffix_sum_matrix(blk, True))


def _sb_sample_kernel(pt_ref, qbd_ref, bias_ref, knew_ref, vnew_ref, *refs, n_kv, rows_per_kv,
                      t_new, n_pg):
    k_refs, v_refs = refs[:n_pg], refs[n_pg:2 * n_pg]
    ucat_ref, o_ref, carry_ref, acc_ref = refs[2 * n_pg:]
    p = pl.program_id(1)
    d, blk = HEAD_DIM_A, PAGE_SIZE
    qbd = qbd_ref[...]
    c = qbd.shape[1]

    def logits(kcat, mask):
        zt = jnp.dot(kcat, qbd, preferred_element_type=F32) * (d ** -0.5) + bias_ref[...]
        l = _neg_softplus(zt)
        if mask is not None:
            l = jnp.where(mask, l, 0.0)
        hi, mid = _split_bf16(l)
        between = jnp.dot(ucat_ref[...], jnp.concatenate([hi, mid], axis=0),
                          preferred_element_type=F32)
        return zt + l + between, jnp.sum(l, axis=0, keepdims=True)

    def pv(at, v_of_kv):
        return jnp.concatenate(
            [jnp.dot(at[kv * rows_per_kv:(kv + 1) * rows_per_kv, :], v_of_kv(kv),
                     preferred_element_type=F32) for kv in range(n_kv)], axis=0)

    @pl.when(p == 0)
    def _new_tokens():
        pad = jnp.zeros((blk - knew_ref.shape[0], knew_ref.shape[1]), F32)
        kn = jnp.concatenate([knew_ref[...], pad], axis=0).astype(BF16)
        vn = jnp.concatenate([vnew_ref[...], pad], axis=0).astype(BF16)
        s_idx = lax.broadcasted_iota(jnp.int32, (blk, c), 0)
        t_idx = lax.broadcasted_iota(jnp.int32, (blk, c), 1) % t_new
        mask = s_idx < t_idx
        logit, tot = logits(kn, mask)
        at = jnp.where(mask, jnp.exp(logit), 0.0).T.astype(BF16)
        acc_ref[...] = pv(at, lambda kv: vn[:, kv * d:(kv + 1) * d])
        carry_ref[...] = jnp.broadcast_to(tot, carry_ref.shape)

    def page_rows(ref, kv):
        return ref[pl.ds(kv, blk, stride=n_kv), :].astype(BF16)

    carry = carry_ref[0:1, :]
    ats = []
    for j in range(n_pg):
        kcat = jnp.concatenate([page_rows(k_refs[j], kv) for kv in range(n_kv)], axis=1)
        logit, tot = logits(kcat, None)
        ats.append(jnp.exp(logit + carry).T.astype(BF16))
        carry = carry + tot
    at_all = jnp.concatenate(ats, axis=1)
    acc_ref[...] += pv(at_all, lambda kv: jnp.concatenate(
        [page_rows(v_refs[j], kv) for j in range(n_pg)], axis=0))
    carry_ref[...] = jnp.broadcast_to(carry, carry_ref.shape)

    @pl.when(p == pl.num_programs(1) - 1)
    def _finish():
        o_ref[...] = acc_ref[...]


def _sb_sample_attention(q, k_new, v_new, k_pool, v_pool, page_table, sb_bias):
    b, t, _ = q.shape
    d, n_kv = HEAD_DIM_A, N_KV_A
    n_heads = q.shape[-1] // d
    group = n_heads // n_kv
    rows_per_kv = group * t
    c = n_kv * rows_per_kv
    n_pages = page_table.shape[1]
    n_pg = 4 if n_pages % 4 == 0 else (2 if n_pages % 2 == 0 else 1)
    q5 = q.reshape(b, t, n_kv, group, d).transpose(0, 2, 4, 3, 1).reshape(b, n_kv, d, rows_per_kv)
    qbd = jnp.einsum("bkdc,kj->bkdjc", q5, jnp.eye(n_kv, dtype=q.dtype))
    qbd = qbd.reshape(b, n_kv * d, c).astype(BF16)
    bias = jnp.repeat(sb_bias.astype(F32), t).reshape(1, c)
    pad_t = 8
    knew = jnp.zeros((b, pad_t, n_kv * d), F32).at[:, :t].set(k_new)
    vnew = jnp.zeros((b, pad_t, n_kv * d), F32).at[:, :t].set(v_new)
    kernel = functools.partial(_sb_sample_kernel, n_kv=n_kv, rows_per_kv=rows_per_kv, t_new=t,
                               n_pg=n_pg)

    def page(j):
        return lambda i, p, pt: (pt[i, n_pages - 1 - (p * n_pg + j)], 0, 0)

    per_b = lambda i, p, pt: (i, 0, 0)
    page_specs = [pl.BlockSpec((None, PAGE_SIZE * n_kv, d), page(j)) for j in range(n_pg)]
    o = pl.pallas_call(
        kernel,
        grid_spec=pltpu.PrefetchScalarGridSpec(
            num_scalar_prefetch=1, grid=(b, n_pages // n_pg),
            in_specs=[
                pl.BlockSpec((None, n_kv * d, c), per_b),
                pl.BlockSpec((1, c), lambda i, p, pt: (0, 0)),
                pl.BlockSpec((None, pad_t, n_kv * d), per_b),
                pl.BlockSpec((None, pad_t, n_kv * d), per_b),
                *page_specs, *page_specs,
                pl.BlockSpec((PAGE_SIZE, 2 * PAGE_SIZE), lambda i, p, pt: (0, 0)),
            ],
            out_specs=pl.BlockSpec((None, c, d), per_b),
            scratch_shapes=[pltpu.VMEM((8, c), F32), pltpu.VMEM((c, d), F32)]),
        out_shape=jax.ShapeDtypeStruct((b, c, d), F32),
        compiler_params=_params(2), name="sb_sample",
    )(page_table, qbd, bias, knew, vnew, *([k_pool] * n_pg), *([v_pool] * n_pg),
      _suffix_sum_matrix(PAGE_SIZE, False))
    return o.reshape(b, n_kv, group, t, d).transpose(0, 3, 1, 2, 4).reshape(b, t, n_heads * d)


def _t5_bucket_table(n_q, n_k, offset):
    dist = offset + np.arange(n_q)[:, None] - np.arange(n_k)[None, :]
    valid = (dist >= 0) & (dist < WINDOW)
    dd = np.maximum(dist, 0)
    max_exact = N_BUCKETS // 2
    far = max_exact + (np.log(np.maximum(dd, 1).astype(np.float32) / np.float32(max_exact))
                       / np.float32(math.log(MAX_DISTANCE / max_exact))
                       * np.float32(N_BUCKETS - max_exact)).astype(np.int32)
    bucket = np.where(dd < max_exact, dd, np.minimum(far, N_BUCKETS - 1))
    return bucket.astype(np.int32), valid


def _sink_softmax_pv(z, sink, v):
    m = jnp.maximum(jnp.max(z, axis=-1, keepdims=True), sink)
    e = jnp.exp(z - m)
    p = e / (jnp.sum(e, axis=-1, keepdims=True) + jnp.exp(sink - m))
    return jnp.dot(p.astype(BF16), v, preferred_element_type=F32)


def _swa_prompt_kernel(sink_ref, q_ref, kprev_ref, kcur_ref, vprev_ref, vcur_ref, bias_ref,
                       valid_ref, o_ref, *, heads_per_step, group):
    n = pl.program_id(1)
    hp = pl.program_id(2)
    d = HEAD_DIM_B
    w = kcur_ref.shape[0]
    col = lax.broadcasted_iota(jnp.int32, (w, 2 * w), 1)
    mask = jnp.logical_and(valid_ref[...] > 0.5, col >= jnp.where(n > 0, 0, w))
    kcat = jnp.concatenate([kprev_ref[...], kcur_ref[...]], axis=0).astype(BF16)
    vcat = jnp.concatenate([vprev_ref[...], vcur_ref[...]], axis=0).astype(BF16)
    for kvl in range(heads_per_step // group):
        k = kcat[:, kvl * d:(kvl + 1) * d]
        v = vcat[:, kvl * d:(kvl + 1) * d]
        for g in range(group):
            hl = kvl * group + g
            q = q_ref[:, hl * d:(hl + 1) * d].astype(BF16)
            z = lax.dot_general(q, k, (((1,), (1,)), ((), ())), preferred_element_type=F32)
            z = z * (d ** -0.5) + bias_ref[hl]
            z = jnp.where(mask, z, -jnp.inf)
            sink = sink_ref[hp * heads_per_step + hl]
            o_ref[:, hl * d:(hl + 1) * d] = _sink_softmax_pv(z, sink, v).astype(o_ref.dtype)


def _swa_prompt_attention(qkv, sinks, bias, valid, *, batch, seq, n_heads):
    d, w = HEAD_DIM_B, WINDOW
    group = n_heads // N_KV_B
    kv_per_step = LANES // d
    hps = kv_per_step * group
    nb = seq // w
    n_steps = n_heads // hps
    q_blocks = n_heads * d // LANES
    k_blocks = N_KV_B * d // LANES
    cur = lambda off: (lambda b, n, hp, s: (b * nb + n, off + hp))
    prev = lambda off: (lambda b, n, hp, s: (b * nb + jnp.maximum(n - 1, 0), off + hp))
    kernel = functools.partial(_swa_prompt_kernel, heads_per_step=hps, group=group)
    return pl.pallas_call(
        kernel,
        grid_spec=pltpu.PrefetchScalarGridSpec(
            num_scalar_prefetch=1, grid=(batch, nb, n_steps),
            in_specs=[
                pl.BlockSpec((w, hps * d), lambda b, n, hp, s: (b * nb + n, hp)),
                pl.BlockSpec((w, LANES), prev(q_blocks)),
                pl.BlockSpec((w, LANES), cur(q_blocks)),
                pl.BlockSpec((w, LANES), prev(q_blocks + k_blocks)),
                pl.BlockSpec((w, LANES), cur(q_blocks + k_blocks)),
                pl.BlockSpec((hps, w, 2 * w), lambda b, n, hp, s: (hp, 0, 0)),
                pl.BlockSpec((w, 2 * w), lambda b, n, hp, s: (0, 0)),
            ],
            out_specs=pl.BlockSpec((w, hps * d), lambda b, n, hp, s: (b * nb + n, hp))),
        out_shape=jax.ShapeDtypeStruct((batch * seq, n_heads * d), BF16),
        compiler_params=_params(3), name="swa_prompt",
    )(sinks, qkv, qkv, qkv, qkv, qkv, bias, valid)


def _swa_sample_kernel(q_ref, kbuf_ref, vbuf_ref, knew_ref, vnew_ref, bias_ref, valid_ref, sink_ref,
                       o_ref, *, n_kv):
    d = HEAD_DIM_B
    kcat = jnp.concatenate([kbuf_ref[...], knew_ref[...]], axis=0).astype(BF16)
    vcat = jnp.concatenate([vbuf_ref[...], vnew_ref[...]], axis=0).astype(BF16)
    mask = valid_ref[...] > 0.5
    for kv in range(n_kv):
        q = q_ref[kv].astype(BF16)
        k = kcat[:, kv * d:(kv + 1) * d]
        v = vcat[:, kv * d:(kv + 1) * d]
        z = lax.dot_general(q, k, (((1,), (1,)), ((), ())), preferred_element_type=F32)
        z = z * (d ** -0.5) + bias_ref[kv]
        z = jnp.where(mask, z, -jnp.inf)
        o_ref[kv] = _sink_softmax_pv(z, sink_ref[kv], v)


def _swa_sample_attention(q, k_new, v_new, k_buf, v_buf, sinks, rel_bias):
    b, t, _ = q.shape
    d, n_kv = HEAD_DIM_B, N_KV_B
    n_heads = q.shape[-1] // d
    group = n_heads // n_kv
    n_buf = k_buf.shape[1]
    pad_t = LANES
    rows = group * t
    bucket, valid = _t5_bucket_table(t, n_buf + pad_t, n_buf)
    valid[:, n_buf + t:] = False
    bias = rel_bias[jnp.asarray(bucket)].astype(F32)
    bias = bias.transpose(2, 0, 1).reshape(n_kv, rows, n_buf + pad_t)
    valid_rows = jnp.asarray(np.tile(valid, (group, 1)).astype(np.float32))
    qr = q.reshape(b, t, n_kv, group, d).transpose(0, 2, 3, 1, 4).reshape(b, n_kv, rows, d)
    knew = jnp.zeros((b, pad_t, n_kv * d), F32).at[:, :t].set(k_new)
    vnew = jnp.zeros((b, pad_t, n_kv * d), F32).at[:, :t].set(v_new)
    sink_rows = jnp.repeat(sinks.astype(F32), t).reshape(n_kv, rows, 1)
    per_b3 = lambda i: (i, 0, 0)
    const3 = lambda i: (0, 0, 0)
    o = pl.pallas_call(
        functools.partial(_swa_sample_kernel, n_kv=n_kv),
        grid=(b,),
        in_specs=[
            pl.BlockSpec((None, n_kv, rows, d), lambda i: (i, 0, 0, 0)),
            pl.BlockSpec((None, n_buf, n_kv * d), per_b3),
            pl.BlockSpec((None, n_buf, n_kv * d), per_b3),
            pl.BlockSpec((None, pad_t, n_kv * d), per_b3),
            pl.BlockSpec((None, pad_t, n_kv * d), per_b3),
            pl.BlockSpec((n_kv, rows, n_buf + pad_t), const3),
            pl.BlockSpec((rows, n_buf + pad_t), lambda i: (0, 0)),
            pl.BlockSpec((n_kv, rows, 1), const3),
        ],
        out_specs=pl.BlockSpec((None, n_kv, rows, d), lambda i: (i, 0, 0, 0)),
        out_shape=jax.ShapeDtypeStruct((b, n_kv, rows, d), F32),
        compiler_params=_params(1), name="swa_sample",
    )(qr, k_buf, v_buf, knew, vnew, bias, valid_rows, sink_rows)
    return o.reshape(b, n_kv, group, t, d).transpose(0, 3, 1, 2, 4).reshape(b, t, n_heads * d)


def _router_kernel(h_ref, w_ref, b_ref, idx_ref, gate_ref, *, n_experts):
    logits = jnp.dot(h_ref[...].astype(BF16), w_ref[...].astype(BF16),
                     preferred_element_type=F32) + b_ref[...]
    lane_i = lax.broadcasted_iota(jnp.int32, logits.shape, 1)
    lane = lane_i.astype(F32)
    neg = -jnp.inf
    lg = jnp.where(lane_i < n_experts, logits, neg)
    m1 = jnp.max(lg, axis=-1, keepdims=True)
    i1 = jnp.min(jnp.where(lg == m1, lane, float(LANES)), axis=-1, keepdims=True)
    lg2 = jnp.where(lane == i1, neg, lg)
    m2 = jnp.max(lg2, axis=-1, keepdims=True)
    i2 = jnp.min(jnp.where(lg2 == m2, lane, float(LANES)), axis=-1, keepdims=True)
    e2 = jnp.exp(m2 - m1)
    den = 1.0 + e2
    idx_ref[...] = jnp.where(lane_i == 0, i1, jnp.where(lane_i == 1, i2, 0.0)).astype(jnp.int32)
    gate_ref[...] = jnp.where(lane_i == 0, 1.0 / den, jnp.where(lane_i == 1, e2 / den, 0.0))


def _router(h, w_router, b_router, *, bm):
    m, k = h.shape
    n_experts = w_router.shape[1]
    w = jnp.zeros((k, LANES), F32).at[:, :n_experts].set(w_router)
    bvec = jnp.zeros((1, LANES), F32).at[0, :n_experts].set(b_router.astype(F32))
    return pl.pallas_call(
        functools.partial(_router_kernel, n_experts=n_experts),
        grid=(m // bm,),
        in_specs=[pl.BlockSpec((bm, k), lambda i: (i, 0)),
                  pl.BlockSpec((k, LANES), lambda i: (0, 0)),
                  pl.BlockSpec((1, LANES), lambda i: (0, 0))],
        out_specs=[pl.BlockSpec((bm, LANES), lambda i: (i, 0))] * 2,
        out_shape=[jax.ShapeDtypeStruct((m, LANES), jnp.int32),
                   jax.ShapeDtypeStruct((m, LANES), F32)],
        compiler_params=_params(1), name="router",
    )(h, w, bvec)


def _row_copy(src_ref, src_row, dst_ref, dst_row, sem):
    return pltpu.make_async_copy(src_ref.at[pl.ds(src_row, 1), :], dst_ref.at[pl.ds(dst_row, 1), :], sem)


def _gather_rows_kernel(idx_ref, src_ref, o_ref, buf_ref, sem, *, rb):
    base = pl.program_id(0) * rb

    def start(j, c):
        _row_copy(src_ref, idx_ref[base + j], buf_ref, j, sem).start()
        return c

    def wait(j, c):
        _row_copy(src_ref, 0, buf_ref, j, sem).wait()
        return c

    lax.fori_loop(0, rb, start, 0)
    lax.fori_loop(0, rb, wait, 0)
    o_ref[...] = buf_ref[...].astype(o_ref.dtype)


def _gather_rows(src, idx, *, rb, out_dtype):
    s, d = idx.shape[0], src.shape[1]
    assert s % rb == 0
    return pl.pallas_call(
        functools.partial(_gather_rows_kernel, rb=rb),
        grid_spec=pltpu.PrefetchScalarGridSpec(
            num_scalar_prefetch=1, grid=(s // rb,),
            in_specs=[pl.BlockSpec(memory_space=pl.ANY)],
            out_specs=pl.BlockSpec((rb, d), lambda i, idx: (i, 0)),
            scratch_shapes=[pltpu.VMEM((rb, d), src.dtype), pltpu.SemaphoreType.DMA(())]),
        out_shape=jax.ShapeDtypeStruct((s, d), out_dtype),
        compiler_params=_params(1), name="moe_gather",
    )(idx, src)


def _combine_kernel(dest_ref, yb_ref, gate_ref, o_ref, buf_ref, sem, *, tb):
    base = pl.program_id(0) * tb

    def start(j, c):
        for k in range(TOP_K):
            _row_copy(yb_ref, dest_ref[(base + j) * TOP_K + k], buf_ref.at[k], j, sem).start()
        return c

    def wait(j, c):
        for k in range(TOP_K):
            _row_copy(yb_ref, 0, buf_ref.at[k], j, sem).wait()
        return c

    lax.fori_loop(0, tb, start, 0)
    lax.fori_loop(0, tb, wait, 0)
    gate = gate_ref[...]
    out = gate[:, 0:1] * buf_ref[0]
    for k in range(1, TOP_K):
        out = out + gate[:, k:k + 1] * buf_ref[k]
    o_ref[...] = out


def _combine(yb, dest, gate, *, tb):
    n, d = gate.shape[0], yb.shape[1]
    assert n % tb == 0
    return pl.pallas_call(
        functools.partial(_combine_kernel, tb=tb),
        grid_spec=pltpu.PrefetchScalarGridSpec(
            num_scalar_prefetch=1, grid=(n // tb,),
            in_specs=[pl.BlockSpec(memory_space=pl.ANY),
                      pl.BlockSpec((tb, LANES), lambda i, dest: (i, 0))],
            out_specs=pl.BlockSpec((tb, d), lambda i, dest: (i, 0)),
            scratch_shapes=[pltpu.VMEM((TOP_K, tb, d), F32), pltpu.SemaphoreType.DMA(())]),
        out_shape=jax.ShapeDtypeStruct((n, d), F32),
        compiler_params=_params(1), name="moe_combine",
    )(dest, yb, gate)


def _moe(h, w_router, b_router, w_gate_e, w_up_e, w_down_e, *, bm_route, bm, bn_up, bn_down, kc_down):
    n, d = h.shape
    n_experts = w_gate_e.shape[0]
    idx, gate = _router(h, w_router, b_router, bm=bm_route)
    flat_e = idx[:, :TOP_K].reshape(-1)
    onehot = (flat_e[:, None] == jnp.arange(n_experts, dtype=jnp.int32)[None, :]).astype(jnp.int32)
    csum = jnp.cumsum(onehot, axis=0)
    rank = jnp.take_along_axis(csum, flat_e[:, None], axis=1)[:, 0] - 1
    counts = csum[-1]
    blocks_e = (counts + bm - 1) // bm
    blocks_end = jnp.cumsum(blocks_e)
    dest = ((blocks_end - blocks_e)[flat_e] * bm + rank).astype(jnp.int32)
    n_blocks = (n * TOP_K) // bm + n_experts
    n_used = blocks_end[-1:].astype(jnp.int32)
    blk = jnp.minimum(jnp.arange(n_blocks, dtype=jnp.int32), n_used[0] - 1)
    blk_e = jnp.minimum(jnp.searchsorted(blocks_end, blk, side="right"), n_experts - 1).astype(jnp.int32)
    slot_tok = jnp.zeros((n_blocks * bm,), jnp.int32).at[dest].set(
        jnp.repeat(jnp.arange(n, dtype=jnp.int32), TOP_K))
    xg = _gather_rows(h, slot_tok, rb=bm, out_dtype=BF16)
    act = _gmm(xg, [w_gate_e, w_up_e], blk_e, n_used, bm=bm, bn=bn_up, kc=d, out_dtype=BF16,
               name="moe_up")
    yb = _gmm(act, [w_down_e], blk_e, n_used, bm=bm, bn=bn_down, kc=kc_down, out_dtype=F32,
              name="moe_down")
    return _combine(yb, dest, gate, tb=LANES)


def kernel(x_prompt, x_sample, c_prompt, c_sample, cache_k_a, cache_v_a, page_table, cache_k_b, cache_v_b, g_mix, g_ffn, g_final, w_ada, b_ada, w_qkv_a, w_o_a, sb_bias_a, w_qkv_b, w_o_b, sinks_b, rel_bias, w_gate_d, w_up_d, w_down_d, w_router, b_router, w_gate_e, w_up_e, w_down_e):
    bp, seq, d = x_prompt.shape
    bs, t, _ = x_sample.shape
    n_p, n_s = bp * seq, bs * t
    n = n_p + n_s
    n_heads_a = d // HEAD_DIM_A
    n_heads_b = d // HEAD_DIM_B
    depth = w_ada.shape[0]
    bm = 832 if n % 832 == 0 else 128
    assert n % bm == 0 and n_p % n_s == 0 and n % LANES == 0

    c_rows = 64
    c_all = jnp.zeros((c_rows, d), F32).at[:bp + bs].set(jnp.concatenate([c_prompt, c_sample], axis=0))
    ada = _gmm(jnp.concatenate([c_all] * depth, axis=0), [w_ada], jnp.arange(depth, dtype=jnp.int32),
               jnp.full((1,), depth, jnp.int32), bm=c_rows, bn=512, kc=d, out_dtype=F32,
               pre_silu=True, name="adaln")
    ada = ada.reshape(depth, c_rows, 6, d)[:, :bp + bs] + b_ada.reshape(depth, 1, 6, d)

    def mods(layer, which):
        m = ada[layer, :, which]
        return m[:bp].reshape(bp, 1, d), jnp.repeat(m[bp:], t, axis=0).reshape(1, n_s, d)

    def norm_both(x, g, scale_of, shift_of, f=None, gate_of=None, h_dtype=BF16):
        sc_p, sc_s = mods(*scale_of)
        sh_p, sh_s = mods(*shift_of)
        gt_p, gt_s = mods(*gate_of) if f is not None else (None, None)
        res = f is not None
        xn, h = _norm(x, g, row0=0, batch=bp, rows=seq, out_rows=n, want_x=res, f=f, gate=gt_p,
                      scale=sc_p, shift=sh_p, h_dtype=h_dtype)
        xn_s, h_s = _norm(x, g, row0=n_p, batch=1, rows=n_s, out_rows=n_s, want_x=res, f=f,
                          gate=gt_s, scale=sc_s, shift=sh_s, h_dtype=h_dtype)
        xn = lax.dynamic_update_slice(xn, xn_s, (n_p, 0)) if res else x
        return xn, lax.dynamic_update_slice(h, h_s, (n_p, 0))

    x = jnp.concatenate([x_prompt.reshape(n_p, d), x_sample.reshape(n_s, d)], axis=0)

    _, h = norm_both(x, g_mix[0], (0, 1), (0, 0))
    qkv_a = _dense_mm(h, [w_qkv_a[0]], bm=bm, bn=512, kc=d, out_dtype=F32, name="qkv_a")
    kv_w = N_KV_A * HEAD_DIM_A
    o = _sb_prompt_attention(qkv_a, sb_bias_a[0].astype(F32), batch=bp, seq=seq, n_heads=n_heads_a)
    qkv_s = qkv_a[n_p:].reshape(bs, t, -1)
    k_pool = cache_k_a[0].reshape(cache_k_a.shape[1], PAGE_SIZE * N_KV_A, HEAD_DIM_A)
    v_pool = cache_v_a[0].reshape(cache_v_a.shape[1], PAGE_SIZE * N_KV_A, HEAD_DIM_A)
    o_s = _sb_sample_attention(qkv_s[..., :d], qkv_s[..., d:d + kv_w], qkv_s[..., d + kv_w:],
                               k_pool, v_pool, page_table, sb_bias_a[0])
    o = jnp.concatenate([o, o_s.reshape(n_s, d).astype(BF16)], axis=0)
    f = _dense_mm(o, [w_o_a[0]], bm=bm, bn=512, kc=d, out_dtype=F32, name="wo_a")
    x, h = norm_both(x, g_ffn[0], (0, 4), (0, 3), f=f, gate_of=(0, 2))
    act = _dense_mm(h, [w_gate_d[0], w_up_d[0]], bm=bm // 2, bn=512, kc=d, out_dtype=BF16, name="ffn_up")
    f = _dense_mm(act, [w_down_d[0]], bm=bm // 2, bn=512, kc=w_down_d.shape[1] // 2, out_dtype=F32,
                  name="ffn_down")

    x, h = norm_both(x, g_mix[1], (1, 1), (1, 0), f=f, gate_of=(0, 5))
    qkv_b = _dense_mm(h, [w_qkv_b[0]], bm=bm, bn=512, kc=d, out_dtype=F32, name="qkv_b")
    kvb_w = N_KV_B * HEAD_DIM_B
    bucket, valid = _t5_bucket_table(WINDOW, 2 * WINDOW, WINDOW)
    band_bias = rel_bias[jnp.asarray(bucket)].astype(F32).transpose(2, 0, 1)
    o = _swa_prompt_attention(qkv_b, sinks_b[0].astype(F32), band_bias,
                              jnp.asarray(valid.astype(np.float32)), batch=bp, seq=seq,
                              n_heads=n_heads_b)
    qkvb_s = qkv_b[n_p:].reshape(bs, t, -1)
    n_buf = cache_k_b.shape[2]
    kb_new, vb_new = qkvb_s[..., d:d + kvb_w], qkvb_s[..., d + kvb_w:]
    o_s = _swa_sample_attention(qkvb_s[..., :d], kb_new, vb_new,
                                cache_k_b[0].reshape(bs, n_buf, kvb_w),
                                cache_v_b[0].reshape(bs, n_buf, kvb_w), sinks_b[0], rel_bias)
    o = jnp.concatenate([o, o_s.reshape(n_s, d).astype(BF16)], axis=0)
    f = _dense_mm(o, [w_o_b[0]], bm=bm, bn=512, kc=d, out_dtype=F32, name="wo_b")
    x, h = norm_both(x, g_ffn[1], (1, 4), (1, 3), f=f, gate_of=(1, 2), h_dtype=F32)
    f = _moe(h, w_router[0], b_router[0], w_gate_e[0], w_up_e[0], w_down_e[0], bm_route=bm,
             bm=256, bn_up=512, bn_down=512, kc_down=w_down_e.shape[2] // 2)

    gt_p, gt_s = mods(1, 5)
    _, y_p = _norm(x, g_final, row0=0, batch=bp, rows=seq, out_rows=n_p, f=f, gate=gt_p, h_dtype=F32)
    _, y_s = _norm(x, g_final, row0=n_p, batch=1, rows=n_s, out_rows=n_s, f=f, gate=gt_s, h_dtype=F32)

    pages = seq // PAGE_SIZE
    k_a_prompt = qkv_a[:n_p, d:d + kv_w].reshape(1, bp, pages, PAGE_SIZE, N_KV_A, HEAD_DIM_A)
    v_a_prompt = qkv_a[:n_p, d + kv_w:].reshape(1, bp, pages, PAGE_SIZE, N_KV_A, HEAD_DIM_A)
    k_a_sample = qkv_s[..., d:d + kv_w].reshape(1, bs, t, N_KV_A, HEAD_DIM_A)
    v_a_sample = qkv_s[..., d + kv_w:].reshape(1, bs, t, N_KV_A, HEAD_DIM_A)
    n_keep = min(WINDOW, seq)

    def last_rows(c0):
        return jnp.stack([qkv_b[(i + 1) * seq - n_keep:(i + 1) * seq, c0:c0 + kvb_w] for i in range(bp)]
                         ).reshape(1, bp, n_keep, N_KV_B, HEAD_DIM_B)

    k_b_prompt, v_b_prompt = last_rows(d), last_rows(d + kvb_w)
    k_b_sample = jnp.concatenate([cache_k_b[0][:, t:], kb_new.reshape(bs, t, N_KV_B, HEAD_DIM_B)], axis=1)[None]
    v_b_sample = jnp.concatenate([cache_v_b[0][:, t:], vb_new.reshape(bs, t, N_KV_B, HEAD_DIM_B)], axis=1)[None]
    return (y_p.reshape(bp, seq, d), y_s.reshape(bs, t, d), k_a_prompt, v_a_prompt, k_a_sample,
            v_a_sample, k_b_prompt, v_b_prompt, k_b_sample, v_b_sample)
```

```python
import functools
import math

import numpy as np
import jax
import jax.numpy as jnp
from jax import lax
from jax.experimental import pallas as pl
from jax.experimental.pallas import tpu as pltpu

F32 = jnp.float32
BF16 = jnp.bfloat16

HEAD_DIM_A = 128
N_KV_A = 8
HEAD_DIM_B = 64
N_KV_B = 8
PAGE_SIZE = 128
WINDOW = 128
N_BUCKETS = 32
MAX_DISTANCE = 128
TOP_K = 2
RMS_EPS = 1e-5

LANES = 128
V7X_VMEM_BYTES = 64 * 1024 * 1024
VMEM_LIMIT = V7X_VMEM_BYTES - 8 * 1024 * 1024
CAST_ROWS = 256


def _params(n_axes):
    return pltpu.CompilerParams(dimension_semantics=("arbitrary",) * n_axes,
                                vmem_limit_bytes=VMEM_LIMIT)


def _sigmoid(x):
    return 1.0 / (1.0 + jnp.exp(-x))


def _gmm_kernel(be_ref, nu_ref, a_ref, *refs, n_w, pre_silu, has_acc):
    w_refs = refs[:n_w]
    acc_ref = refs[n_w] if has_acc else None
    o_ref = refs[n_w + has_acc]
    wb_refs = refs[n_w + has_acc + 1:]
    r = pl.program_id(1)
    e = be_ref[r]
    prev = be_ref[jnp.maximum(r - 1, 0)]

    @pl.when(jnp.logical_or(r == 0, e != prev))
    def _convert_weights():
        kc = wb_refs[0].shape[0]

        def body(i, c):
            rows = pl.ds(pl.multiple_of(i * CAST_ROWS, CAST_ROWS), CAST_ROWS)
            for w_ref, wb_ref in zip(w_refs, wb_refs):
                wb_ref[rows, :] = w_ref[0, rows, :].astype(BF16)
            return c

        lax.fori_loop(0, kc // CAST_ROWS, body, 0)

    @pl.when(r < nu_ref[0])
    def _compute():
        a = a_ref[...]
        if pre_silu:
            a = a.astype(F32)
            a = a * _sigmoid(a)
        a = a.astype(BF16)
        if n_w == 2:
            g = jnp.dot(a, wb_refs[0][...], preferred_element_type=F32)
            u = jnp.dot(a, wb_refs[1][...], preferred_element_type=F32)
            res = g * _sigmoid(g) * u
        else:
            res = jnp.dot(a, wb_refs[0][...], preferred_element_type=F32)
        if has_acc:
            res = acc_ref[...] + res
        o_ref[...] = res.astype(o_ref.dtype)

    @pl.when(r >= nu_ref[0])
    def _unused_block():
        o_ref[...] = acc_ref[...] if has_acc else jnp.zeros_like(o_ref)


def _gmm(a, ws, blk_e, n_used, *, bm, bn, kc, out_dtype, pre_silu=False, name="gmm"):
    m, k = a.shape
    _, _, n = ws[0].shape
    assert m % bm == 0 and n % bn == 0 and k % kc == 0 and kc % CAST_ROWS == 0
    n_w = len(ws)
    out = None
    for kh in range(k // kc):
        has_acc = out is not None
        in_specs = [pl.BlockSpec((bm, kc), lambda j, r, be, nu, kh=kh: (r, kh))]
        in_specs += [pl.BlockSpec((1, kc, bn), lambda j, r, be, nu, kh=kh: (be[r], kh, j))
                     for _ in ws]
        io_spec = pl.BlockSpec((bm, bn), lambda j, r, be, nu: (r, j))
        args = [blk_e, n_used, a, *ws]
        if has_acc:
            in_specs.append(io_spec)
            args.append(out)
        out = pl.pallas_call(
            functools.partial(_gmm_kernel, n_w=n_w, pre_silu=pre_silu, has_acc=has_acc),
            grid_spec=pltpu.PrefetchScalarGridSpec(
                num_scalar_prefetch=2, grid=(n // bn, m // bm), in_specs=in_specs,
                out_specs=io_spec, scratch_shapes=[pltpu.VMEM((kc, bn), BF16) for _ in ws]),
            out_shape=jax.ShapeDtypeStruct((m, n), out_dtype),
            input_output_aliases={3 + n_w: 0} if has_acc else {},
            compiler_params=_params(2), name=name,
        )(*args)
    return out


def _dense_mm(a, ws, *, bm, bn, kc, out_dtype, pre_silu=False, name="mm"):
    nb = a.shape[0] // bm
    return _gmm(a, [w[None] for w in ws], jnp.zeros((nb,), jnp.int32),
                jnp.full((1,), nb, jnp.int32), bm=bm, bn=bn, kc=kc, out_dtype=out_dtype,
                pre_silu=pre_silu, name=name)


def _norm_kernel(*refs, has_res, has_mod, want_x):
    it = iter(refs)
    x = next(it)[...]
    if has_res:
        f_ref = next(it)
        gate_ref = next(it)
        x = x + gate_ref[...] * f_ref[...]
    g_ref = next(it)
    if has_mod:
        sc_ref = next(it)
        sh_ref = next(it)
    if want_x:
        next(it)[...] = x
    h_ref = next(it)
    y = x * lax.rsqrt(jnp.mean(x * x, axis=-1, keepdims=True) + RMS_EPS) * g_ref[...]
    if has_mod:
        y = y * (1.0 + sc_ref[...]) + sh_ref[...]
    h_ref[...] = y.astype(h_ref.dtype)


def _norm(x, g, *, row0, batch, rows, out_rows, want_x=False, f=None, gate=None, scale=None,
          shift=None, h_dtype=BF16, ts=LANES, name="norm"):
    n, d = x.shape
    ts = min(ts, rows)
    assert rows % ts == 0 and row0 % ts == 0 and n % ts == 0
    nt = rows // ts
    has_res, has_mod = f is not None, scale is not None
    assert has_res or not want_x
    in_blk0 = row0 // ts
    full = out_rows == n
    out_blk0 = in_blk0 if full else 0
    n_blocks = (n - row0) // ts if full else batch * nt
    in_spec = pl.BlockSpec((ts, d), lambda i: (in_blk0 + i, 0))
    out_spec = pl.BlockSpec((ts, d), lambda i: (out_blk0 + i, 0))

    def mod_spec(v):
        if v.shape[1] == 1:
            return pl.BlockSpec((None, 1, d), lambda i: (jnp.minimum(i // nt, batch - 1), 0, 0))
        assert batch == 1
        return pl.BlockSpec((None, ts, d), lambda i: (0, i, 0))

    args, specs = [x], [in_spec]
    if has_res:
        args += [f, gate]
        specs += [in_spec, mod_spec(gate)]
    args.append(g.reshape(1, d))
    specs.append(pl.BlockSpec((1, d), lambda i: (0, 0)))
    if has_mod:
        args += [scale, shift]
        specs += [mod_spec(scale), mod_spec(shift)]
    out_shape, out_specs = [], []
    if want_x:
        out_shape.append(jax.ShapeDtypeStruct((out_rows, d), F32))
        out_specs.append(out_spec)
    out_shape.append(jax.ShapeDtypeStruct((out_rows, d), h_dtype))
    out_specs.append(out_spec)
    outs = pl.pallas_call(
        functools.partial(_norm_kernel, has_res=has_res, has_mod=has_mod, want_x=want_x),
        grid=(n_blocks,), in_specs=specs, out_specs=out_specs, out_shape=out_shape,
        compiler_params=_params(1), name=name,
    )(*args)
    return (outs[0], outs[1]) if want_x else (None, outs[0])


def _neg_softplus(z):
    return -(jnp.maximum(z, 0.0) + jnp.log(1.0 + jnp.exp(-jnp.abs(z))))


def _split_bf16(l):
    hi = l.astype(BF16)
    mid = (l - hi.astype(F32)).astype(BF16)
    return hi, mid


def _suffix_sum_matrix(blk):
    j = np.arange(blk)
    strict = (j[:, None] > j[None, :]).astype(np.float32)
    half = np.concatenate([strict, np.ones((blk, blk), np.float32)], axis=1)
    return jnp.asarray(np.concatenate([half, half], axis=0), BF16)


def _newer_key_matrix(blk):
    s = np.arange(blk)
    u = (s[None, :] > s[:, None]).astype(np.float32)
    return jnp.asarray(np.concatenate([u, u], axis=1), BF16)


def _sb_prompt_kernel(sb_ref, q_ref, k_ref, v_ref, umat_ref, o_ref, kb_ref, vb_ref, qb_ref,
                      bias_ref, carry_ref, acc_ref, *, group, blk):
    kv = pl.program_id(1)
    i = pl.program_id(2)
    d = HEAD_DIM_A

    @pl.when(i == 0)
    def _per_kv_head_setup():
        kb_ref[...] = k_ref[...].astype(BF16)
        vb_ref[...] = v_ref[...].astype(BF16)
        for g in range(group):
            bias_ref[g * blk:(g + 1) * blk, :] = jnp.full((blk, 2 * blk), sb_ref[kv * group + g], F32)

    for g in range(group):
        qb_ref[g * blk:(g + 1) * blk, :] = q_ref[:, g * d:(g + 1) * d].astype(BF16)
    rows = lax.broadcasted_iota(jnp.int32, (group * blk, blk), 0)
    cols = lax.broadcasted_iota(jnp.int32, (group * blk, blk), 1)
    causal = cols < (rows & (blk - 1))
    nt_dims = (((1,), (1,)), ((), ()))

    def block(off, mask, first):
        sl = pl.ds(pl.multiple_of(off, blk), blk)
        z = lax.dot_general(qb_ref[...], kb_ref[sl, :], nt_dims, preferred_element_type=F32)
        z = z * (d ** -0.5) + bias_ref[:, :blk]
        l = _neg_softplus(z)
        if mask is not None:
            l = jnp.where(mask, l, 0.0)
        hi, mid = _split_bf16(l)
        bt = jnp.dot(jnp.concatenate([hi, mid], axis=1), umat_ref[...], preferred_element_type=F32)
        logit = z + l + bt[:, :blk]
        if not first:
            logit = logit + carry_ref[...]
        a = jnp.exp(logit)
        if mask is not None:
            a = jnp.where(mask, a, 0.0)
        pv = jnp.dot(a.astype(BF16), vb_ref[sl, :], preferred_element_type=F32)
        if first:
            acc_ref[...] = pv
            carry_ref[...] = bt[:, blk:]
        else:
            acc_ref[...] += pv
            carry_ref[...] += bt[:, blk:]

    def pair_scores(off):
        sl = pl.ds(pl.multiple_of(off, blk), 2 * blk)
        z = lax.dot_general(qb_ref[...], kb_ref[sl, :], nt_dims, preferred_element_type=F32)
        z = z * (d ** -0.5) + bias_ref[...]
        l = _neg_softplus(z)
        hi, mid = _split_bf16(l)
        umat = umat_ref[...]
        bt_old = jnp.dot(jnp.concatenate([hi[:, :blk], mid[:, :blk]], axis=1), umat,
                         preferred_element_type=F32)
        bt_new = jnp.dot(jnp.concatenate([hi[:, blk:], mid[:, blk:]], axis=1), umat,
                         preferred_element_type=F32)
        return sl, z + l, bt_old, bt_new

    def pair_apply(scores, carry):
        sl, zl, bt_old, bt_new = scores
        newer = jnp.concatenate([bt_old[:, :blk] + bt_new[:, blk:] + carry, bt_new[:, :blk] + carry],
                                axis=1)
        pv = jnp.dot(jnp.exp(zl + newer).astype(BF16), vb_ref[sl, :], preferred_element_type=F32)
        return pv, carry + bt_new[:, blk:] + bt_old[:, blk:]

    def block_pairs(offs):
        scores = [pair_scores(off) for off in offs]
        carry = carry_ref[...]
        acc = acc_ref[...]
        for s in scores:
            pv, carry = pair_apply(s, carry)
            acc = acc + pv
        acc_ref[...] = acc
        carry_ref[...] = carry

    block(i * blk, causal, True)

    def older_quad(j, c):
        block_pairs([(i - 2 - 4 * j) * blk, (i - 4 - 4 * j) * blk])
        return c

    lax.fori_loop(0, lax.shift_right_logical(i, 2), older_quad, 0)
    rest = i & 3

    @pl.when((rest & 2) == 2)
    def _oldest_pair():
        block_pairs([(rest & 1) * blk])

    @pl.when((rest & 1) == 1)
    def _oldest_block():
        block(0, None, False)

    for g in range(group):
        o_ref[:, g * d:(g + 1) * d] = acc_ref[g * blk:(g + 1) * blk, :].astype(o_ref.dtype)


def _sb_prompt_attention(qkv, sb_bias, *, batch, seq, n_heads):
    d, blk = HEAD_DIM_A, PAGE_SIZE
    assert blk & (blk - 1) == 0
    group = n_heads // N_KV_A
    nq = seq // blk
    kernel = functools.partial(_sb_prompt_kernel, group=group, blk=blk)
    return pl.pallas_call(
        kernel,
        grid_spec=pltpu.PrefetchScalarGridSpec(
            num_scalar_prefetch=1, grid=(batch, N_KV_A, nq),
            in_specs=[
                pl.BlockSpec((blk, group * d), lambda b, kv, i, sb: (b * nq + i, kv)),
                pl.BlockSpec((seq, d), lambda b, kv, i, sb: (b, n_heads + kv)),
                pl.BlockSpec((seq, d), lambda b, kv, i, sb: (b, n_heads + N_KV_A + kv)),
                pl.BlockSpec((2 * blk, 2 * blk), lambda b, kv, i, sb: (0, 0)),
            ],
            out_specs=pl.BlockSpec((blk, group * d), lambda b, kv, i, sb: (b * nq + i, kv)),
            scratch_shapes=[pltpu.VMEM((seq, d), BF16), pltpu.VMEM((seq, d), BF16),
                            pltpu.VMEM((group * blk, d), BF16), pltpu.VMEM((group * blk, 2 * blk), F32),
                            pltpu.VMEM((group * blk, blk), F32), pltpu.VMEM((group * blk, d), F32)]),
        out_shape=jax.ShapeDtypeStruct((batch * seq, n_heads * d), BF16),
        compiler_params=_params(3), name="sb_prompt",
    )(sb_bias, qkv, qkv, qkv, _suffix_sum_matrix(blk))


def _sb_sample_kernel(pt_ref, qbd_ref, bias_ref, knew_ref, vnew_ref, *refs, n_kv, rows_per_kv,
                      t_new, n_pg):
    k_refs, v_refs = refs[:n_pg], refs[n_pg:2 * n_pg]
    u_ref, o_ref, carry_ref, acc_ref = refs[2 * n_pg:]
    p = pl.program_id(1)
    d, blk = HEAD_DIM_A, PAGE_SIZE
    qbd = qbd_ref[...]
    c = qbd.shape[1]

    def attend(kcat, v_of_kv, mask, carry):
        zt = jnp.dot(kcat, qbd, preferred_element_type=F32) * (d ** -0.5) + bias_ref[...]
        l = _neg_softplus(zt)
        if mask is not None:
            l = jnp.where(mask, l, 0.0)
        hi, mid = _split_bf16(l)
        newer = []
        for j in range(kcat.shape[0] // blk):
            rows = slice(j * blk, (j + 1) * blk)
            between = jnp.dot(u_ref[...], jnp.concatenate([hi[rows], mid[rows]], axis=0),
                              preferred_element_type=F32)
            newer.append(between + carry)
            carry = carry + jnp.sum(l[rows], axis=0, keepdims=True)
        a = jnp.exp(zt + l + jnp.concatenate(newer, axis=0))
        if mask is not None:
            a = jnp.where(mask, a, 0.0)
        at = a.T.astype(BF16)
        out = jnp.concatenate(
            [jnp.dot(at[kv * rows_per_kv:(kv + 1) * rows_per_kv, :], v_of_kv(kv),
                     preferred_element_type=F32) for kv in range(n_kv)], axis=0)
        return out, carry

    @pl.when(p == 0)
    def _new_tokens():
        pad = jnp.zeros((blk - knew_ref.shape[0], knew_ref.shape[1]), F32)
        kn = jnp.concatenate([knew_ref[...], pad], axis=0).astype(BF16)
        vn = jnp.concatenate([vnew_ref[...], pad], axis=0).astype(BF16)
        s_idx = lax.broadcasted_iota(jnp.int32, (blk, c), 0)
        t_idx = lax.broadcasted_iota(jnp.int32, (blk, c), 1) % t_new
        out, carry = attend(kn, lambda kv: vn[:, kv * d:(kv + 1) * d], s_idx < t_idx,
                            jnp.zeros((1, c), F32))
        acc_ref[...] = out
        carry_ref[...] = jnp.broadcast_to(carry, carry_ref.shape)

    def kv_rows(page_refs, kv):
        return jnp.concatenate([r[pl.ds(kv, blk, stride=n_kv), :].astype(BF16) for r in page_refs],
                               axis=0)

    kcat = jnp.concatenate([kv_rows(k_refs, kv) for kv in range(n_kv)], axis=1)
    out, carry = attend(kcat, lambda kv: kv_rows(v_refs, kv), None, carry_ref[0:1, :])
    acc_ref[...] += out
    carry_ref[...] = jnp.broadcast_to(carry, carry_ref.shape)

    @pl.when(p == pl.num_programs(1) - 1)
    def _finish():
        o_ref[...] = acc_ref[...]


def _sb_sample_attention(q, k_new, v_new, k_pool, v_pool, page_table, sb_bias):
    b, t, _ = q.shape
    d, n_kv = HEAD_DIM_A, N_KV_A
    n_heads = q.shape[-1] // d
    group = n_heads // n_kv
    rows_per_kv = group * t
    c = n_kv * rows_per_kv
    n_pages = page_table.shape[1]
    n_pg = 4 if n_pages % 4 == 0 else (2 if n_pages % 2 == 0 else 1)
    q5 = q.reshape(b, t, n_kv, group, d).transpose(0, 2, 4, 3, 1).reshape(b, n_kv, d, rows_per_kv)
    qbd = jnp.einsum("bkdc,kj->bkdjc", q5, jnp.eye(n_kv, dtype=q.dtype))
    qbd = qbd.reshape(b, n_kv * d, c).astype(BF16)
    bias = jnp.repeat(sb_bias.astype(F32), t).reshape(1, c)
    pad_t = 8
    knew = jnp.zeros((b, pad_t, n_kv * d), F32).at[:, :t].set(k_new)
    vnew = jnp.zeros((b, pad_t, n_kv * d), F32).at[:, :t].set(v_new)
    kernel = functools.partial(_sb_sample_kernel, n_kv=n_kv, rows_per_kv=rows_per_kv, t_new=t,
                               n_pg=n_pg)

    def page(j):
        return lambda i, p, pt: (pt[i, n_pages - 1 - (p * n_pg + j)], 0, 0)

    per_b = lambda i, p, pt: (i, 0, 0)
    page_specs = [pl.BlockSpec((None, PAGE_SIZE * n_kv, d), page(j)) for j in range(n_pg)]
    o = pl.pallas_call(
        kernel,
        grid_spec=pltpu.PrefetchScalarGridSpec(
            num_scalar_prefetch=1, grid=(b, n_pages // n_pg),
            in_specs=[
                pl.BlockSpec((None, n_kv * d, c), per_b),
                pl.BlockSpec((1, c), lambda i, p, pt: (0, 0)),
                pl.BlockSpec((None, pad_t, n_kv * d), per_b),
                pl.BlockSpec((None, pad_t, n_kv * d), per_b),
                *page_specs, *page_specs,
                pl.BlockSpec((PAGE_SIZE, 2 * PAGE_SIZE), lambda i, p, pt: (0, 0)),
            ],
            out_specs=pl.BlockSpec((None, c, d), per_b),
            scratch_shapes=[pltpu.VMEM((8, c), F32), pltpu.VMEM((c, d), F32)]),
        out_shape=jax.ShapeDtypeStruct((b, c, d), F32),
        compiler_params=_params(2), name="sb_sample",
    )(page_table, qbd, bias, knew, vnew, *([k_pool] * n_pg), *([v_pool] * n_pg),
      _newer_key_matrix(PAGE_SIZE))
    return o.reshape(b, n_kv, group, t, d).transpose(0, 3, 1, 2, 4).reshape(b, t, n_heads * d)


def _t5_bucket_table(n_q, n_k, offset):
    dist = offset + np.arange(n_q)[:, None] - np.arange(n_k)[None, :]
    valid = (dist >= 0) & (dist < WINDOW)
    dd = np.maximum(dist, 0)
    max_exact = N_BUCKETS // 2
    far = max_exact + (np.log(np.maximum(dd, 1).astype(np.float32) / np.float32(max_exact))
                       / np.float32(math.log(MAX_DISTANCE / max_exact))
                       * np.float32(N_BUCKETS - max_exact)).astype(np.int32)
    bucket = np.where(dd < max_exact, dd, np.minimum(far, N_BUCKETS - 1))
    return bucket.astype(np.int32), valid


def _sink_softmax_pv(z, sink, v):
    m = jnp.maximum(jnp.max(z, axis=-1, keepdims=True), sink)
    e = jnp.exp(z - m)
    p = e / (jnp.sum(e, axis=-1, keepdims=True) + jnp.exp(sink - m))
    return jnp.dot(p.astype(BF16), v, preferred_element_type=F32)


def _swa_prompt_kernel(sink_ref, q_ref, kprev_ref, kcur_ref, vprev_ref, vcur_ref, bias_ref,
                       valid_ref, o_ref, *, heads_per_step, group):
    n = pl.program_id(1)
    hp = pl.program_id(2)
    d = HEAD_DIM_B
    w = kcur_ref.shape[0]
    col = lax.broadcasted_iota(jnp.int32, (w, 2 * w), 1)
    mask = jnp.logical_and(valid_ref[...] > 0.5, col >= jnp.where(n > 0, 0, w))
    kcat = jnp.concatenate([kprev_ref[...], kcur_ref[...]], axis=0).astype(BF16)
    vcat = jnp.concatenate([vprev_ref[...], vcur_ref[...]], axis=0).astype(BF16)
    for kvl in range(heads_per_step // group):
        k = kcat[:, kvl * d:(kvl + 1) * d]
        v = vcat[:, kvl * d:(kvl + 1) * d]
        for g in range(group):
            hl = kvl * group + g
            q = q_ref[:, hl * d:(hl + 1) * d].astype(BF16)
            z = lax.dot_general(q, k, (((1,), (1,)), ((), ())), preferred_element_type=F32)
            z = z * (d ** -0.5) + bias_ref[hl]
            z = jnp.where(mask, z, -jnp.inf)
            sink = sink_ref[hp * heads_per_step + hl]
            o_ref[:, hl * d:(hl + 1) * d] = _sink_softmax_pv(z, sink, v).astype(o_ref.dtype)


def _swa_prompt_attention(qkv, sinks, bias, valid, *, batch, seq, n_heads):
    d, w = HEAD_DIM_B, WINDOW
    group = n_heads // N_KV_B
    kv_per_step = LANES // d
    hps = kv_per_step * group
    nb = seq // w
    n_steps = n_heads // hps
    q_blocks = n_heads * d // LANES
    k_blocks = N_KV_B * d // LANES
    cur = lambda off: (lambda b, n, hp, s: (b * nb + n, off + hp))
    prev = lambda off: (lambda b, n, hp, s: (b * nb + jnp.maximum(n - 1, 0), off + hp))
    kernel = functools.partial(_swa_prompt_kernel, heads_per_step=hps, group=group)
    return pl.pallas_call(
        kernel,
        grid_spec=pltpu.PrefetchScalarGridSpec(
            num_scalar_prefetch=1, grid=(batch, nb, n_steps),
            in_specs=[
                pl.BlockSpec((w, hps * d), lambda b, n, hp, s: (b * nb + n, hp)),
                pl.BlockSpec((w, LANES), prev(q_blocks)),
                pl.BlockSpec((w, LANES), cur(q_blocks)),
                pl.BlockSpec((w, LANES), prev(q_blocks + k_blocks)),
                pl.BlockSpec((w, LANES), cur(q_blocks + k_blocks)),
                pl.BlockSpec((hps, w, 2 * w), lambda b, n, hp, s: (hp, 0, 0)),
                pl.BlockSpec((w, 2 * w), lambda b, n, hp, s: (0, 0)),
            ],
            out_specs=pl.BlockSpec((w, hps * d), lambda b, n, hp, s: (b * nb + n, hp))),
        out_shape=jax.ShapeDtypeStruct((batch * seq, n_heads * d), BF16),
        compiler_params=_params(3), name="swa_prompt",
    )(sinks, qkv, qkv, qkv, qkv, qkv, bias, valid)


def _swa_sample_kernel(q_ref, kbuf_ref, vbuf_ref, knew_ref, vnew_ref, bias_ref, valid_ref, sink_ref,
                       o_ref, *, n_kv):
    d = HEAD_DIM_B
    kcat = jnp.concatenate([kbuf_ref[...], knew_ref[...]], axis=0).astype(BF16)
    vcat = jnp.concatenate([vbuf_ref[...], vnew_ref[...]], axis=0).astype(BF16)
    mask = valid_ref[...] > 0.5
    for kv in range(n_kv):
        q = q_ref[kv].astype(BF16)
        k = kcat[:, kv * d:(kv + 1) * d]
        v = vcat[:, kv * d:(kv + 1) * d]
        z = lax.dot_general(q, k, (((1,), (1,)), ((), ())), preferred_element_type=F32)
        z = z * (d ** -0.5) + bias_ref[kv]
        z = jnp.where(mask, z, -jnp.inf)
        o_ref[kv] = _sink_softmax_pv(z, sink_ref[kv], v)


def _swa_sample_attention(q, k_new, v_new, k_buf, v_buf, sinks, rel_bias):
    b, t, _ = q.shape
    d, n_kv = HEAD_DIM_B, N_KV_B
    n_heads = q.shape[-1] // d
    group = n_heads // n_kv
    n_buf = k_buf.shape[1]
    pad_t = LANES
    rows = group * t
    bucket, valid = _t5_bucket_table(t, n_buf + pad_t, n_buf)
    valid[:, n_buf + t:] = False
    bias = rel_bias[jnp.asarray(bucket)].astype(F32)
    bias = bias.transpose(2, 0, 1).reshape(n_kv, rows, n_buf + pad_t)
    valid_rows = jnp.asarray(np.tile(valid, (group, 1)).astype(np.float32))
    qr = q.reshape(b, t, n_kv, group, d).transpose(0, 2, 3, 1, 4).reshape(b, n_kv, rows, d)
    knew = jnp.zeros((b, pad_t, n_kv * d), F32).at[:, :t].set(k_new)
    vnew = jnp.zeros((b, pad_t, n_kv * d), F32).at[:, :t].set(v_new)
    sink_rows = jnp.repeat(sinks.astype(F32), t).reshape(n_kv, rows, 1)
    per_b3 = lambda i: (i, 0, 0)
    const3 = lambda i: (0, 0, 0)
    o = pl.pallas_call(
        functools.partial(_swa_sample_kernel, n_kv=n_kv),
        grid=(b,),
        in_specs=[
            pl.BlockSpec((None, n_kv, rows, d), lambda i: (i, 0, 0, 0)),
            pl.BlockSpec((None, n_buf, n_kv * d), per_b3),
            pl.BlockSpec((None, n_buf, n_kv * d), per_b3),
            pl.BlockSpec((None, pad_t, n_kv * d), per_b3),
            pl.BlockSpec((None, pad_t, n_kv * d), per_b3),
            pl.BlockSpec((n_kv, rows, n_buf + pad_t), const3),
            pl.BlockSpec((rows, n_buf + pad_t), lambda i: (0, 0)),
            pl.BlockSpec((n_kv, rows, 1), const3),
        ],
        out_specs=pl.BlockSpec((None, n_kv, rows, d), lambda i: (i, 0, 0, 0)),
        out_shape=jax.ShapeDtypeStruct((b, n_kv, rows, d), F32),
        compiler_params=_params(1), name="swa_sample",
    )(qr, k_buf, v_buf, knew, vnew, bias, valid_rows, sink_rows)
    return o.reshape(b, n_kv, group, t, d).transpose(0, 3, 1, 2, 4).reshape(b, t, n_heads * d)


def _router_kernel(h_ref, w_ref, b_ref, idx_ref, gate_ref, *, n_experts):
    logits = jnp.dot(h_ref[...].astype(BF16), w_ref[...].astype(BF16),
                     preferred_element_type=F32) + b_ref[...]
    lane_i = lax.broadcasted_iota(jnp.int32, logits.shape, 1)
    lane = lane_i.astype(F32)
    neg = -jnp.inf
    lg = jnp.where(lane_i < n_experts, logits, neg)
    m1 = jnp.max(lg, axis=-1, keepdims=True)
    i1 = jnp.min(jnp.where(lg == m1, lane, float(LANES)), axis=-1, keepdims=True)
    lg2 = jnp.where(lane == i1, neg, lg)
    m2 = jnp.max(lg2, axis=-1, keepdims=True)
    i2 = jnp.min(jnp.where(lg2 == m2, lane, float(LANES)), axis=-1, keepdims=True)
    e2 = jnp.exp(m2 - m1)
    den = 1.0 + e2
    idx_ref[...] = jnp.where(lane_i == 0, i1, jnp.where(lane_i == 1, i2, 0.0)).astype(jnp.int32)
    gate_ref[...] = jnp.where(lane_i == 0, 1.0 / den, jnp.where(lane_i == 1, e2 / den, 0.0))


def _router(h, w_router, b_router, *, bm):
    m, k = h.shape
    n_experts = w_router.shape[1]
    w = jnp.zeros((k, LANES), F32).at[:, :n_experts].set(w_router)
    bvec = jnp.zeros((1, LANES), F32).at[0, :n_experts].set(b_router.astype(F32))
    return pl.pallas_call(
        functools.partial(_router_kernel, n_experts=n_experts),
        grid=(m // bm,),
        in_specs=[pl.BlockSpec((bm, k), lambda i: (i, 0)),
                  pl.BlockSpec((k, LANES), lambda i: (0, 0)),
                  pl.BlockSpec((1, LANES), lambda i: (0, 0))],
        out_specs=[pl.BlockSpec((bm, LANES), lambda i: (i, 0))] * 2,
        out_shape=[jax.ShapeDtypeStruct((m, LANES), jnp.int32),
                   jax.ShapeDtypeStruct((m, LANES), F32)],
        compiler_params=_params(1), name="router",
    )(h, w, bvec)


def _row_copy(src_ref, src_row, dst_ref, dst_row, sem):
    return pltpu.make_async_copy(src_ref.at[pl.ds(src_row, 1), :], dst_ref.at[pl.ds(dst_row, 1), :], sem)


def _gather_rows_kernel(idx_ref, src_ref, o_ref, buf_ref, sem, *, rb):
    i = pl.program_id(0)

    def issue(step, slot):
        def start(j, c):
            _row_copy(src_ref, idx_ref[step * rb + j], buf_ref.at[slot], j, sem.at[slot]).start()
            return c

        lax.fori_loop(0, rb, start, 0)

    @pl.when(i == 0)
    def _first_block():
        issue(0, 0)

    @pl.when(i + 1 < pl.num_programs(0))
    def _next_block():
        issue(i + 1, (i + 1) & 1)

    slot = i & 1

    def wait(j, c):
        _row_copy(src_ref, 0, buf_ref.at[slot], j, sem.at[slot]).wait()
        return c

    lax.fori_loop(0, rb, wait, 0)
    o_ref[...] = buf_ref[slot].astype(o_ref.dtype)


def _gather_rows(src, idx, *, rb, out_dtype):
    s, d = idx.shape[0], src.shape[1]
    assert s % rb == 0
    return pl.pallas_call(
        functools.partial(_gather_rows_kernel, rb=rb),
        grid_spec=pltpu.PrefetchScalarGridSpec(
            num_scalar_prefetch=1, grid=(s // rb,),
            in_specs=[pl.BlockSpec(memory_space=pl.ANY)],
            out_specs=pl.BlockSpec((rb, d), lambda i, idx: (i, 0)),
            scratch_shapes=[pltpu.VMEM((2, rb, d), src.dtype), pltpu.SemaphoreType.DMA((2,))]),
        out_shape=jax.ShapeDtypeStruct((s, d), out_dtype),
        compiler_params=_params(1), name="moe_gather",
    )(idx, src)


def _combine_kernel(dest_ref, yb_ref, gate_ref, o_ref, buf_ref, sem, *, tb):
    base = pl.program_id(0) * tb

    def start(j, c):
        for k in range(TOP_K):
            _row_copy(yb_ref, dest_ref[(base + j) * TOP_K + k], buf_ref.at[k], j, sem).start()
        return c

    def wait(j, c):
        for k in range(TOP_K):
            _row_copy(yb_ref, 0, buf_ref.at[k], j, sem).wait()
        return c

    lax.fori_loop(0, tb, start, 0)
    lax.fori_loop(0, tb, wait, 0)
    gate = gate_ref[...]
    out = gate[:, 0:1] * buf_ref[0]
    for k in range(1, TOP_K):
        out = out + gate[:, k:k + 1] * buf_ref[k]
    o_ref[...] = out


def _combine(yb, dest, gate, *, tb):
    n, d = gate.shape[0], yb.shape[1]
    assert n % tb == 0
    return pl.pallas_call(
        functools.partial(_combine_kernel, tb=tb),
        grid_spec=pltpu.PrefetchScalarGridSpec(
            num_scalar_prefetch=1, grid=(n // tb,),
            in_specs=[pl.BlockSpec(memory_space=pl.ANY),
                      pl.BlockSpec((tb, LANES), lambda i, dest: (i, 0))],
            out_specs=pl.BlockSpec((tb, d), lambda i, dest: (i, 0)),
            scratch_shapes=[pltpu.VMEM((TOP_K, tb, d), F32), pltpu.SemaphoreType.DMA(())]),
        out_shape=jax.ShapeDtypeStruct((n, d), F32),
        compiler_params=_params(1), name="moe_combine",
    )(dest, yb, gate)


def _moe(h, w_router, b_router, w_gate_e, w_up_e, w_down_e, *, bm_route, bm, bn_up, bn_down, kc_down):
    n, d = h.shape
    n_experts = w_gate_e.shape[0]
    idx, gate = _router(h, w_router, b_router, bm=bm_route)
    flat_e = idx[:, :TOP_K].reshape(-1)
    onehot = (flat_e[:, None] == jnp.arange(n_experts, dtype=jnp.int32)[None, :]).astype(jnp.int32)
    csum = jnp.cumsum(onehot, axis=0)
    rank = jnp.take_along_axis(csum, flat_e[:, None], axis=1)[:, 0] - 1
    counts = csum[-1]
    blocks_e = (counts + bm - 1) // bm
    blocks_end = jnp.cumsum(blocks_e)
    dest = ((blocks_end - blocks_e)[flat_e] * bm + rank).astype(jnp.int32)
    n_blocks = (n * TOP_K) // bm + n_experts
    n_used = blocks_end[-1:].astype(jnp.int32)
    blk = jnp.minimum(jnp.arange(n_blocks, dtype=jnp.int32), n_used[0] - 1)
    blk_e = jnp.minimum(jnp.searchsorted(blocks_end, blk, side="right"), n_experts - 1).astype(jnp.int32)
    slot_tok = jnp.zeros((n_blocks * bm,), jnp.int32).at[dest].set(
        jnp.repeat(jnp.arange(n, dtype=jnp.int32), TOP_K))
    xg = _gather_rows(h, slot_tok, rb=2 * LANES, out_dtype=BF16)
    act = _gmm(xg, [w_gate_e, w_up_e], blk_e, n_used, bm=bm, bn=bn_up, kc=d, out_dtype=BF16,
               name="moe_up")
    yb = _gmm(act, [w_down_e], blk_e, n_used, bm=bm, bn=bn_down, kc=kc_down, out_dtype=F32,
              name="moe_down")
    return _combine(yb, dest, gate, tb=LANES)


def kernel(x_prompt, x_sample, c_prompt, c_sample, cache_k_a, cache_v_a, page_table, cache_k_b, cache_v_b, g_mix, g_ffn, g_final, w_ada, b_ada, w_qkv_a, w_o_a, sb_bias_a, w_qkv_b, w_o_b, sinks_b, rel_bias, w_gate_d, w_up_d, w_down_d, w_router, b_router, w_gate_e, w_up_e, w_down_e):
    bp, seq, d = x_prompt.shape
    bs, t, _ = x_sample.shape
    n_p, n_s = bp * seq, bs * t
    n = n_p + n_s
    n_heads_a = d // HEAD_DIM_A
    n_heads_b = d // HEAD_DIM_B
    depth = w_ada.shape[0]
    bm = 832 if n % 832 == 0 else 128
    assert n % bm == 0 and n_p % n_s == 0 and n % LANES == 0

    c_rows = 64
    c_all = jnp.zeros((c_rows, d), F32).at[:bp + bs].set(jnp.concatenate([c_prompt, c_sample], axis=0))
    ada = _gmm(jnp.concatenate([c_all] * depth, axis=0), [w_ada], jnp.arange(depth, dtype=jnp.int32),
               jnp.full((1,), depth, jnp.int32), bm=c_rows, bn=512, kc=d, out_dtype=F32,
               pre_silu=True, name="adaln")
    ada = ada.reshape(depth, c_rows, 6, d)[:, :bp + bs] + b_ada.reshape(depth, 1, 6, d)

    def mods(layer, which):
        m = ada[layer, :, which]
        return m[:bp].reshape(bp, 1, d), jnp.repeat(m[bp:], t, axis=0).reshape(1, n_s, d)

    def norm_both(x, g, scale_of, shift_of, f=None, gate_of=None, h_dtype=BF16):
        sc_p, sc_s = mods(*scale_of)
        sh_p, sh_s = mods(*shift_of)
        gt_p, gt_s = mods(*gate_of) if f is not None else (None, None)
        res = f is not None
        xn, h = _norm(x, g, row0=0, batch=bp, rows=seq, out_rows=n, want_x=res, f=f, gate=gt_p,
                      scale=sc_p, shift=sh_p, h_dtype=h_dtype)
        xn_s, h_s = _norm(x, g, row0=n_p, batch=1, rows=n_s, out_rows=n_s, want_x=res, f=f,
                          gate=gt_s, scale=sc_s, shift=sh_s, h_dtype=h_dtype)
        xn = lax.dynamic_update_slice(xn, xn_s, (n_p, 0)) if res else x
        return xn, lax.dynamic_update_slice(h, h_s, (n_p, 0))

    x = jnp.concatenate([x_prompt.reshape(n_p, d), x_sample.reshape(n_s, d)], axis=0)

    _, h = norm_both(x, g_mix[0], (0, 1), (0, 0))
    qkv_a = _dense_mm(h, [w_qkv_a[0]], bm=bm, bn=512, kc=d, out_dtype=F32, name="qkv_a")
    kv_w = N_KV_A * HEAD_DIM_A
    o = _sb_prompt_attention(qkv_a, sb_bias_a[0].astype(F32), batch=bp, seq=seq, n_heads=n_heads_a)
    qkv_s = qkv_a[n_p:].reshape(bs, t, -1)
    k_pool = cache_k_a[0].reshape(cache_k_a.shape[1], PAGE_SIZE * N_KV_A, HEAD_DIM_A)
    v_pool = cache_v_a[0].reshape(cache_v_a.shape[1], PAGE_SIZE * N_KV_A, HEAD_DIM_A)
    o_s = _sb_sample_attention(qkv_s[..., :d], qkv_s[..., d:d + kv_w], qkv_s[..., d + kv_w:],
                               k_pool, v_pool, page_table, sb_bias_a[0])
    o = jnp.concatenate([o, o_s.reshape(n_s, d).astype(BF16)], axis=0)
    f = _dense_mm(o, [w_o_a[0]], bm=bm, bn=512, kc=d, out_dtype=F32, name="wo_a")
    x, h = norm_both(x, g_ffn[0], (0, 4), (0, 3), f=f, gate_of=(0, 2))
    act = _dense_mm(h, [w_gate_d[0], w_up_d[0]], bm=bm // 2, bn=512, kc=d, out_dtype=BF16, name="ffn_up")
    f = _dense_mm(act, [w_down_d[0]], bm=bm // 2, bn=512, kc=w_down_d.shape[1] // 2, out_dtype=F32,
                  name="ffn_down")

    x, h = norm_both(x, g_mix[1], (1, 1), (1, 0), f=f, gate_of=(0, 5))
    qkv_b = _dense_mm(h, [w_qkv_b[0]], bm=bm, bn=512, kc=d, out_dtype=F32, name="qkv_b")
    kvb_w = N_KV_B * HEAD_DIM_B
    bucket, valid = _t5_bucket_table(WINDOW, 2 * WINDOW, WINDOW)
    band_bias = rel_bias[jnp.asarray(bucket)].astype(F32).transpose(2, 0, 1)
    o = _swa_prompt_attention(qkv_b, sinks_b[0].astype(F32), band_bias,
                              jnp.asarray(valid.astype(np.float32)), batch=bp, seq=seq,
                              n_heads=n_heads_b)
    qkvb_s = qkv_b[n_p:].reshape(bs, t, -1)
    n_buf = cache_k_b.shape[2]
    kb_new, vb_new = qkvb_s[..., d:d + kvb_w], qkvb_s[..., d + kvb_w:]
    o_s = _swa_sample_attention(qkvb_s[..., :d], kb_new, vb_new,
                                cache_k_b[0].reshape(bs, n_buf, kvb_w),
                                cache_v_b[0].reshape(bs, n_buf, kvb_w), sinks_b[0], rel_bias)
    o = jnp.concatenate([o, o_s.reshape(n_s, d).astype(BF16)], axis=0)
    f = _dense_mm(o, [w_o_b[0]], bm=bm, bn=512, kc=d, out_dtype=F32, name="wo_b")
    x, h = norm_both(x, g_ffn[1], (1, 4), (1, 3), f=f, gate_of=(1, 2), h_dtype=F32)
    f = _moe(h, w_router[0], b_router[0], w_gate_e[0], w_up_e[0], w_down_e[0], bm_route=bm,
             bm=512, bn_up=512, bn_down=512, kc_down=w_down_e.shape[2] // 2)

    gt_p, gt_s = mods(1, 5)
    _, y_p = _norm(x, g_final, row0=0, batch=bp, rows=seq, out_rows=n_p, f=f, gate=gt_p, h_dtype=F32)
    _, y_s = _norm(x, g_final, row0=n_p, batch=1, rows=n_s, out_rows=n_s, f=f, gate=gt_s, h_dtype=F32)

    pages = seq // PAGE_SIZE
    k_a_prompt = qkv_a[:n_p, d:d + kv_w].reshape(1, bp, pages, PAGE_SIZE, N_KV_A, HEAD_DIM_A)
    v_a_prompt = qkv_a[:n_p, d + kv_w:].reshape(1, bp, pages, PAGE_SIZE, N_KV_A, HEAD_DIM_A)
    k_a_sample = qkv_s[..., d:d + kv_w].reshape(1, bs, t, N_KV_A, HEAD_DIM_A)
    v_a_sample = qkv_s[..., d + kv_w:].reshape(1, bs, t, N_KV_A, HEAD_DIM_A)
    n_keep = min(WINDOW, seq)

    def last_rows(c0):
        return jnp.stack([qkv_b[(i + 1) * seq - n_keep:(i + 1) * seq, c0:c0 + kvb_w] for i in range(bp)]
                         ).reshape(1, bp, n_keep, N_KV_B, HEAD_DIM_B)

    k_b_prompt, v_b_prompt = last_rows(d), last_rows(d + kvb_w)
    k_b_sample = jnp.concatenate([cache_k_b[0][:, t:], kb_new.reshape(bs, t, N_KV_B, HEAD_DIM_B)], axis=1)[None]
    v_b_sample = jnp.concatenate([cache_v_b[0][:, t:], vb_new.reshape(bs, t, N_KV_B, HEAD_DIM_B)], axis=1)[None]
    return (y_p.reshape(bp, seq, d), y_s.reshape(bs, t, d), k_a_prompt, v_a_prompt, k_a_sample,
            v_a_sample, k_b_prompt, v_b_prompt, k_b_sample, v_b_sample)
```

```python
import functools
import math

import numpy as np
import jax
import jax.numpy as jnp
from jax import lax
from jax.experimental import pallas as pl
from jax.experimental.pallas import tpu as pltpu

F32 = jnp.float32
BF16 = jnp.bfloat16

HEAD_DIM_A = 128
N_KV_A = 8
HEAD_DIM_B = 64
N_KV_B = 8
PAGE_SIZE = 128
WINDOW = 128
N_BUCKETS = 32
MAX_DISTANCE = 128
TOP_K = 2
RMS_EPS = 1e-5

LANES = 128
V7X_VMEM_BYTES = 64 * 1024 * 1024
VMEM_LIMIT = V7X_VMEM_BYTES - 8 * 1024 * 1024
CAST_ROWS = 256
DMA_UNROLL = 8


def _params(n_axes):
    return pltpu.CompilerParams(dimension_semantics=("arbitrary",) * n_axes,
                                vmem_limit_bytes=VMEM_LIMIT)


def _sigmoid(x):
    return 1.0 / (1.0 + jnp.exp(-x))


def _gmm_kernel(be_ref, live_ref, a_ref, *refs, n_w, pre_silu, has_acc, half_blocks):
    w_refs = refs[:n_w]
    acc_ref = refs[n_w] if has_acc else None
    o_ref = refs[n_w + has_acc]
    wb_refs = refs[n_w + has_acc + 1:]
    r = pl.program_id(1)
    e = be_ref[r]
    prev = be_ref[jnp.maximum(r - 1, 0)]
    bm = a_ref.shape[0]
    live = live_ref[r]

    @pl.when(jnp.logical_or(r == 0, e != prev))
    def _convert_weights():
        kc = wb_refs[0].shape[0]

        def body(i, c):
            rows = pl.ds(pl.multiple_of(i * CAST_ROWS, CAST_ROWS), CAST_ROWS)
            for w_ref, wb_ref in zip(w_refs, wb_refs):
                wb_ref[rows, :] = w_ref[0, rows, :].astype(BF16)
            return c

        lax.fori_loop(0, kc // CAST_ROWS, body, 0)

    def passthrough(rows):
        o_ref[rows, :] = acc_ref[rows, :] if has_acc else jnp.zeros_like(o_ref[rows, :])

    def compute(n_rows):
        rows = slice(0, n_rows)
        a = a_ref[rows, :]
        if pre_silu:
            a = a.astype(F32)
            a = a * _sigmoid(a)
        a = a.astype(BF16)
        if n_w == 2:
            g = jnp.dot(a, wb_refs[0][...], preferred_element_type=F32)
            u = jnp.dot(a, wb_refs[1][...], preferred_element_type=F32)
            res = g * _sigmoid(g) * u
        else:
            res = jnp.dot(a, wb_refs[0][...], preferred_element_type=F32)
        if has_acc:
            res = acc_ref[rows, :] + res
        o_ref[rows, :] = res.astype(o_ref.dtype)
        if n_rows < bm:
            passthrough(slice(n_rows, bm))

    if half_blocks:
        @pl.when(live > bm // 2)
        def _full_block():
            compute(bm)

        @pl.when(jnp.logical_and(live > 0, live <= bm // 2))
        def _half_block():
            compute(bm // 2)
    else:
        @pl.when(live > 0)
        def _block():
            compute(bm)

    @pl.when(live == 0)
    def _unused_block():
        passthrough(slice(0, bm))


def _gmm(a, ws, blk_e, live, *, bm, bn, kc, out_dtype, pre_silu=False, half_blocks=False,
         name="gmm"):
    m, k = a.shape
    _, _, n = ws[0].shape
    assert m % bm == 0 and n % bn == 0 and k % kc == 0 and kc % CAST_ROWS == 0
    n_w = len(ws)
    out = None
    for kh in range(k // kc):
        has_acc = out is not None
        in_specs = [pl.BlockSpec((bm, kc), lambda j, r, be, lv, kh=kh: (r, kh))]
        in_specs += [pl.BlockSpec((1, kc, bn), lambda j, r, be, lv, kh=kh: (be[r], kh, j))
                     for _ in ws]
        io_spec = pl.BlockSpec((bm, bn), lambda j, r, be, lv: (r, j))
        args = [blk_e, live, a, *ws]
        if has_acc:
            in_specs.append(io_spec)
            args.append(out)
        out = pl.pallas_call(
            functools.partial(_gmm_kernel, n_w=n_w, pre_silu=pre_silu, has_acc=has_acc,
                              half_blocks=half_blocks),
            grid_spec=pltpu.PrefetchScalarGridSpec(
                num_scalar_prefetch=2, grid=(n // bn, m // bm), in_specs=in_specs,
                out_specs=io_spec, scratch_shapes=[pltpu.VMEM((kc, bn), BF16) for _ in ws]),
            out_shape=jax.ShapeDtypeStruct((m, n), out_dtype),
            input_output_aliases={3 + n_w: 0} if has_acc else {},
            compiler_params=_params(2), name=name,
        )(*args)
    return out


def _dense_mm(a, ws, *, bm, bn, kc, out_dtype, pre_silu=False, name="mm"):
    nb = a.shape[0] // bm
    return _gmm(a, [w[None] for w in ws], jnp.zeros((nb,), jnp.int32),
                jnp.full((nb,), bm, jnp.int32), bm=bm, bn=bn, kc=kc, out_dtype=out_dtype,
                pre_silu=pre_silu, name=name)


def _norm_kernel(*refs, has_res, has_mod, want_x):
    it = iter(refs)
    x = next(it)[...]
    if has_res:
        f_ref = next(it)
        gate_ref = next(it)
        x = x + gate_ref[...] * f_ref[...]
    g_ref = next(it)
    if has_mod:
        sc_ref = next(it)
        sh_ref = next(it)
    if want_x:
        next(it)[...] = x
    h_ref = next(it)
    y = x * lax.rsqrt(jnp.mean(x * x, axis=-1, keepdims=True) + RMS_EPS) * g_ref[...]
    if has_mod:
        y = y * (1.0 + sc_ref[...]) + sh_ref[...]
    h_ref[...] = y.astype(h_ref.dtype)


def _norm(x, g, *, row0, batch, rows, out_rows, want_x=False, f=None, gate=None, scale=None,
          shift=None, h_dtype=BF16, ts=LANES, name="norm"):
    n, d = x.shape
    ts = min(ts, rows)
    assert rows % ts == 0 and row0 % ts == 0 and n % ts == 0
    nt = rows // ts
    has_res, has_mod = f is not None, scale is not None
    assert has_res or not want_x
    in_blk0 = row0 // ts
    full = out_rows == n
    out_blk0 = in_blk0 if full else 0
    n_blocks = (n - row0) // ts if full else batch * nt
    in_spec = pl.BlockSpec((ts, d), lambda i: (in_blk0 + i, 0))
    out_spec = pl.BlockSpec((ts, d), lambda i: (out_blk0 + i, 0))

    def mod_spec(v):
        if v.shape[1] == 1:
            return pl.BlockSpec((None, 1, d), lambda i: (jnp.minimum(i // nt, batch - 1), 0, 0))
        assert batch == 1
        return pl.BlockSpec((None, ts, d), lambda i: (0, i, 0))

    args, specs = [x], [in_spec]
    if has_res:
        args += [f, gate]
        specs += [in_spec, mod_spec(gate)]
    args.append(g.reshape(1, d))
    specs.append(pl.BlockSpec((1, d), lambda i: (0, 0)))
    if has_mod:
        args += [scale, shift]
        specs += [mod_spec(scale), mod_spec(shift)]
    out_shape, out_specs = [], []
    if want_x:
        out_shape.append(jax.ShapeDtypeStruct((out_rows, d), F32))
        out_specs.append(out_spec)
    out_shape.append(jax.ShapeDtypeStruct((out_rows, d), h_dtype))
    out_specs.append(out_spec)
    outs = pl.pallas_call(
        functools.partial(_norm_kernel, has_res=has_res, has_mod=has_mod, want_x=want_x),
        grid=(n_blocks,), in_specs=specs, out_specs=out_specs, out_shape=out_shape,
        compiler_params=_params(1), name=name,
    )(*args)
    return (outs[0], outs[1]) if want_x else (None, outs[0])


def _neg_softplus(z):
    return -(jnp.maximum(z, 0.0) + jnp.log(1.0 + jnp.exp(-jnp.abs(z))))


def _split_bf16(l):
    hi = l.astype(BF16)
    mid = (l - hi.astype(F32)).astype(BF16)
    return hi, mid


def _suffix_sum_matrix(blk):
    j = np.arange(blk)
    strict = (j[:, None] > j[None, :]).astype(np.float32)
    half = np.concatenate([strict, np.ones((blk, blk), np.float32)], axis=1)
    return jnp.asarray(np.concatenate([half, half], axis=0), BF16)


def _newer_key_matrix(blk):
    s = np.arange(blk)
    u = (s[None, :] > s[:, None]).astype(np.float32)
    return jnp.asarray(np.concatenate([u, u], axis=1), BF16)


def _sb_prompt_kernel(sb_ref, q_ref, k_ref, v_ref, umat_ref, o_ref, kb_ref, vb_ref, qb_ref,
                      bias_ref, carry_ref, acc_ref, *, group, blk):
    kv = pl.program_id(1)
    i = pl.program_id(2)
    d = HEAD_DIM_A

    @pl.when(i == 0)
    def _per_kv_head_setup():
        kb_ref[...] = k_ref[...].astype(BF16)
        vb_ref[...] = v_ref[...].astype(BF16)
        for g in range(group):
            bias_ref[g * blk:(g + 1) * blk, :] = jnp.full((blk, 2 * blk), sb_ref[kv * group + g], F32)

    for g in range(group):
        qb_ref[g * blk:(g + 1) * blk, :] = q_ref[:, g * d:(g + 1) * d].astype(BF16)
    rows = lax.broadcasted_iota(jnp.int32, (group * blk, blk), 0)
    cols = lax.broadcasted_iota(jnp.int32, (group * blk, blk), 1)
    causal = cols < (rows & (blk - 1))
    nt_dims = (((1,), (1,)), ((), ()))

    def block(off, mask, first):
        sl = pl.ds(pl.multiple_of(off, blk), blk)
        z = lax.dot_general(qb_ref[...], kb_ref[sl, :], nt_dims, preferred_element_type=F32)
        z = z * (d ** -0.5) + bias_ref[:, :blk]
        l = _neg_softplus(z)
        if mask is not None:
            l = jnp.where(mask, l, 0.0)
        hi, mid = _split_bf16(l)
        bt = jnp.dot(jnp.concatenate([hi, mid], axis=1), umat_ref[...], preferred_element_type=F32)
        logit = z + l + bt[:, :blk]
        if not first:
            logit = logit + carry_ref[...]
        a = jnp.exp(logit)
        if mask is not None:
            a = jnp.where(mask, a, 0.0)
        pv = jnp.dot(a.astype(BF16), vb_ref[sl, :], preferred_element_type=F32)
        if first:
            acc_ref[...] = pv
            carry_ref[...] = bt[:, blk:]
        else:
            acc_ref[...] += pv
            carry_ref[...] += bt[:, blk:]

    def pair_scores(off):
        sl = pl.ds(pl.multiple_of(off, blk), 2 * blk)
        z = lax.dot_general(qb_ref[...], kb_ref[sl, :], nt_dims, preferred_element_type=F32)
        z = z * (d ** -0.5) + bias_ref[...]
        l = _neg_softplus(z)
        hi, mid = _split_bf16(l)
        umat = umat_ref[...]
        bt_old = jnp.dot(jnp.concatenate([hi[:, :blk], mid[:, :blk]], axis=1), umat,
                         preferred_element_type=F32)
        bt_new = jnp.dot(jnp.concatenate([hi[:, blk:], mid[:, blk:]], axis=1), umat,
                         preferred_element_type=F32)
        return sl, z + l, bt_old, bt_new

    def pair_apply(scores, carry):
        sl, zl, bt_old, bt_new = scores
        newer = jnp.concatenate([bt_old[:, :blk] + bt_new[:, blk:] + carry, bt_new[:, :blk] + carry],
                                axis=1)
        pv = jnp.dot(jnp.exp(zl + newer).astype(BF16), vb_ref[sl, :], preferred_element_type=F32)
        return pv, carry + bt_new[:, blk:] + bt_old[:, blk:]

    def block_pairs(offs):
        scores = [pair_scores(off) for off in offs]
        carry = carry_ref[...]
        acc = acc_ref[...]
        for s in scores:
            pv, carry = pair_apply(s, carry)
            acc = acc + pv
        acc_ref[...] = acc
        carry_ref[...] = carry

    block(i * blk, causal, True)

    def older_quad(j, c):
        block_pairs([(i - 2 - 4 * j) * blk, (i - 4 - 4 * j) * blk])
        return c

    lax.fori_loop(0, lax.shift_right_logical(i, 2), older_quad, 0)
    rest = i & 3

    @pl.when((rest & 2) == 2)
    def _oldest_pair():
        block_pairs([(rest & 1) * blk])

    @pl.when((rest & 1) == 1)
    def _oldest_block():
        block(0, None, False)

    for g in range(group):
        o_ref[:, g * d:(g + 1) * d] = acc_ref[g * blk:(g + 1) * blk, :].astype(o_ref.dtype)


def _sb_prompt_attention(qkv, sb_bias, *, batch, seq, n_heads):
    d, blk = HEAD_DIM_A, PAGE_SIZE
    assert blk & (blk - 1) == 0
    group = n_heads // N_KV_A
    nq = seq // blk
    kernel = functools.partial(_sb_prompt_kernel, group=group, blk=blk)
    return pl.pallas_call(
        kernel,
        grid_spec=pltpu.PrefetchScalarGridSpec(
            num_scalar_prefetch=1, grid=(batch, N_KV_A, nq),
            in_specs=[
                pl.BlockSpec((blk, group * d), lambda b, kv, i, sb: (b * nq + i, kv)),
                pl.BlockSpec((seq, d), lambda b, kv, i, sb: (b, n_heads + kv)),
                pl.BlockSpec((seq, d), lambda b, kv, i, sb: (b, n_heads + N_KV_A + kv)),
                pl.BlockSpec((2 * blk, 2 * blk), lambda b, kv, i, sb: (0, 0)),
            ],
            out_specs=pl.BlockSpec((blk, group * d), lambda b, kv, i, sb: (b * nq + i, kv)),
            scratch_shapes=[pltpu.VMEM((seq, d), BF16), pltpu.VMEM((seq, d), BF16),
                            pltpu.VMEM((group * blk, d), BF16), pltpu.VMEM((group * blk, 2 * blk), F32),
                            pltpu.VMEM((group * blk, blk), F32), pltpu.VMEM((group * blk, d), F32)]),
        out_shape=jax.ShapeDtypeStruct((batch * seq, n_heads * d), BF16),
        compiler_params=_params(3), name="sb_prompt",
    )(sb_bias, qkv, qkv, qkv, _suffix_sum_matrix(blk))


def _sb_sample_kernel(pt_ref, qbd_ref, bias_ref, knew_ref, vnew_ref, *refs, n_kv, rows_per_kv,
                      t_new, n_pg):
    k_refs, v_refs = refs[:n_pg], refs[n_pg:2 * n_pg]
    u_ref, o_ref, carry_ref, acc_ref, logw_ref = refs[2 * n_pg:]
    p = pl.program_id(1)
    d, blk = HEAD_DIM_A, PAGE_SIZE
    qbd = qbd_ref[...]
    c = qbd.shape[1]

    def log_weights(kcat, mask, carry):
        zt = jnp.dot(kcat, qbd, preferred_element_type=F32) * (d ** -0.5) + bias_ref[...]
        l = _neg_softplus(zt)
        if mask is not None:
            l = jnp.where(mask, l, 0.0)
        hi, mid = _split_bf16(l)
        newer = []
        for j in range(kcat.shape[0] // blk):
            rows = slice(j * blk, (j + 1) * blk)
            between = jnp.dot(u_ref[...], jnp.concatenate([hi[rows], mid[rows]], axis=0),
                              preferred_element_type=F32)
            newer.append(between + carry)
            carry = carry + jnp.sum(l[rows], axis=0, keepdims=True)
        return zt + l + jnp.concatenate(newer, axis=0), carry

    def weighted_values(logw, v_of_kv, mask):
        a = jnp.exp(logw)
        if mask is not None:
            a = jnp.where(mask, a, 0.0)
        at = a.T.astype(BF16)
        return jnp.concatenate(
            [jnp.dot(at[kv * rows_per_kv:(kv + 1) * rows_per_kv, :], v_of_kv(kv),
                     preferred_element_type=F32) for kv in range(n_kv)], axis=0)

    @pl.when(p == 0)
    def _new_tokens():
        pad = jnp.zeros((blk - knew_ref.shape[0], knew_ref.shape[1]), F32)
        kn = jnp.concatenate([knew_ref[...], pad], axis=0).astype(BF16)
        vn = jnp.concatenate([vnew_ref[...], pad], axis=0).astype(BF16)
        s_idx = lax.broadcasted_iota(jnp.int32, (blk, c), 0)
        t_idx = lax.broadcasted_iota(jnp.int32, (blk, c), 1) % t_new
        mask = s_idx < t_idx
        logw, carry = log_weights(kn, mask, jnp.zeros((1, c), F32))
        acc_ref[...] = weighted_values(logw, lambda kv: vn[:, kv * d:(kv + 1) * d], mask)
        carry_ref[...] = jnp.broadcast_to(carry, carry_ref.shape)
        logw_ref[...] = jnp.full(logw_ref.shape, -jnp.inf, F32)

    def kv_rows(page_refs, kv):
        return jnp.concatenate([r[pl.ds(kv, blk, stride=n_kv), :].astype(BF16) for r in page_refs],
                               axis=0)

    out = weighted_values(logw_ref[...], lambda kv: kv_rows(v_refs, kv), None)
    kcat = jnp.concatenate([kv_rows(k_refs, kv) for kv in range(n_kv)], axis=1)
    logw, carry = log_weights(kcat, None, carry_ref[0:1, :])
    acc_ref[...] += out
    logw_ref[...] = logw
    carry_ref[...] = jnp.broadcast_to(carry, carry_ref.shape)

    @pl.when(p == pl.num_programs(1) - 1)
    def _finish():
        o_ref[...] = acc_ref[...]


def _sb_sample_attention(q, k_new, v_new, k_pool, v_pool, page_table, sb_bias):
    b, t, _ = q.shape
    d, n_kv = HEAD_DIM_A, N_KV_A
    n_heads = q.shape[-1] // d
    group = n_heads // n_kv
    rows_per_kv = group * t
    c = n_kv * rows_per_kv
    n_pages = page_table.shape[1]
    n_pg = 4 if n_pages % 4 == 0 else (2 if n_pages % 2 == 0 else 1)
    q5 = q.reshape(b, t, n_kv, group, d).transpose(0, 2, 4, 3, 1).reshape(b, n_kv, d, rows_per_kv)
    qbd = jnp.einsum("bkdc,kj->bkdjc", q5, jnp.eye(n_kv, dtype=q.dtype))
    qbd = qbd.reshape(b, n_kv * d, c).astype(BF16)
    bias = jnp.repeat(sb_bias.astype(F32), t).reshape(1, c)
    pad_t = 8
    knew = jnp.zeros((b, pad_t, n_kv * d), F32).at[:, :t].set(k_new)
    vnew = jnp.zeros((b, pad_t, n_kv * d), F32).at[:, :t].set(v_new)
    kernel = functools.partial(_sb_sample_kernel, n_kv=n_kv, rows_per_kv=rows_per_kv, t_new=t,
                               n_pg=n_pg)

    n_steps = n_pages // n_pg

    def page(j, lag):
        def index(i, p, pt):
            step = jnp.clip(p - lag, 0, n_steps - 1)
            return pt[i, n_pages - 1 - (step * n_pg + j)], 0, 0
        return index

    per_b = lambda i, p, pt: (i, 0, 0)
    page_specs = lambda lag: [pl.BlockSpec((None, PAGE_SIZE * n_kv, d), page(j, lag))
                              for j in range(n_pg)]
    o = pl.pallas_call(
        kernel,
        grid_spec=pltpu.PrefetchScalarGridSpec(
            num_scalar_prefetch=1, grid=(b, n_steps + 1),
            in_specs=[
                pl.BlockSpec((None, n_kv * d, c), per_b),
                pl.BlockSpec((1, c), lambda i, p, pt: (0, 0)),
                pl.BlockSpec((None, pad_t, n_kv * d), per_b),
                pl.BlockSpec((None, pad_t, n_kv * d), per_b),
                *page_specs(0), *page_specs(1),
                pl.BlockSpec((PAGE_SIZE, 2 * PAGE_SIZE), lambda i, p, pt: (0, 0)),
            ],
            out_specs=pl.BlockSpec((None, c, d), per_b),
            scratch_shapes=[pltpu.VMEM((8, c), F32), pltpu.VMEM((c, d), F32),
                            pltpu.VMEM((n_pg * PAGE_SIZE, c), F32)]),
        out_shape=jax.ShapeDtypeStruct((b, c, d), F32),
        compiler_params=_params(2), name="sb_sample",
    )(page_table, qbd, bias, knew, vnew, *([k_pool] * n_pg), *([v_pool] * n_pg),
      _newer_key_matrix(PAGE_SIZE))
    return o.reshape(b, n_kv, group, t, d).transpose(0, 3, 1, 2, 4).reshape(b, t, n_heads * d)


def _t5_bucket_table(n_q, n_k, offset):
    dist = offset + np.arange(n_q)[:, None] - np.arange(n_k)[None, :]
    valid = (dist >= 0) & (dist < WINDOW)
    dd = np.maximum(dist, 0)
    max_exact = N_BUCKETS // 2
    far = max_exact + (np.log(np.maximum(dd, 1).astype(np.float32) / np.float32(max_exact))
                       / np.float32(math.log(MAX_DISTANCE / max_exact))
                       * np.float32(N_BUCKETS - max_exact)).astype(np.int32)
    bucket = np.where(dd < max_exact, dd, np.minimum(far, N_BUCKETS - 1))
    return bucket.astype(np.int32), valid


def _sink_softmax_pv(z, sink, v):
    m = jnp.maximum(jnp.max(z, axis=-1, keepdims=True), sink)
    e = jnp.exp(z - m)
    p = e / (jnp.sum(e, axis=-1, keepdims=True) + jnp.exp(sink - m))
    return jnp.dot(p.astype(BF16), v, preferred_element_type=F32)


def _swa_prompt_kernel(sink_ref, q_ref, kprev_ref, kcur_ref, vprev_ref, vcur_ref, bias_ref,
                       valid_ref, o_ref, *, heads_per_step, group):
    n = pl.program_id(1)
    hp = pl.program_id(2)
    d = HEAD_DIM_B
    w = kcur_ref.shape[0]
    col = lax.broadcasted_iota(jnp.int32, (w, 2 * w), 1)
    mask = jnp.logical_and(valid_ref[...] > 0.5, col >= jnp.where(n > 0, 0, w))
    kcat = jnp.concatenate([kprev_ref[...], kcur_ref[...]], axis=0).astype(BF16)
    vcat = jnp.concatenate([vprev_ref[...], vcur_ref[...]], axis=0).astype(BF16)
    for kvl in range(heads_per_step // group):
        k = kcat[:, kvl * d:(kvl + 1) * d]
        v = vcat[:, kvl * d:(kvl + 1) * d]
        for g in range(group):
            hl = kvl * group + g
            q = q_ref[:, hl * d:(hl + 1) * d].astype(BF16)
            z = lax.dot_general(q, k, (((1,), (1,)), ((), ())), preferred_element_type=F32)
            z = z * (d ** -0.5) + bias_ref[hl]
            z = jnp.where(mask, z, -jnp.inf)
            sink = sink_ref[hp * heads_per_step + hl]
            o_ref[:, hl * d:(hl + 1) * d] = _sink_softmax_pv(z, sink, v).astype(o_ref.dtype)


def _swa_prompt_attention(qkv, sinks, bias, valid, *, batch, seq, n_heads):
    d, w = HEAD_DIM_B, WINDOW
    group = n_heads // N_KV_B
    kv_per_step = LANES // d
    hps = kv_per_step * group
    nb = seq // w
    n_steps = n_heads // hps
    q_blocks = n_heads * d // LANES
    k_blocks = N_KV_B * d // LANES
    cur = lambda off: (lambda b, n, hp, s: (b * nb + n, off + hp))
    prev = lambda off: (lambda b, n, hp, s: (b * nb + jnp.maximum(n - 1, 0), off + hp))
    kernel = functools.partial(_swa_prompt_kernel, heads_per_step=hps, group=group)
    return pl.pallas_call(
        kernel,
        grid_spec=pltpu.PrefetchScalarGridSpec(
            num_scalar_prefetch=1, grid=(batch, nb, n_steps),
            in_specs=[
                pl.BlockSpec((w, hps * d), lambda b, n, hp, s: (b * nb + n, hp)),
                pl.BlockSpec((w, LANES), prev(q_blocks)),
                pl.BlockSpec((w, LANES), cur(q_blocks)),
                pl.BlockSpec((w, LANES), prev(q_blocks + k_blocks)),
                pl.BlockSpec((w, LANES), cur(q_blocks + k_blocks)),
                pl.BlockSpec((hps, w, 2 * w), lambda b, n, hp, s: (hp, 0, 0)),
                pl.BlockSpec((w, 2 * w), lambda b, n, hp, s: (0, 0)),
            ],
            out_specs=pl.BlockSpec((w, hps * d), lambda b, n, hp, s: (b * nb + n, hp))),
        out_shape=jax.ShapeDtypeStruct((batch * seq, n_heads * d), BF16),
        compiler_params=_params(3), name="swa_prompt",
    )(sinks, qkv, qkv, qkv, qkv, qkv, bias, valid)


def _swa_sample_kernel(q_ref, kbuf_ref, vbuf_ref, knew_ref, vnew_ref, bias_ref, valid_ref, sink_ref,
                       o_ref, *, n_kv):
    d = HEAD_DIM_B
    kcat = jnp.concatenate([kbuf_ref[...], knew_ref[...]], axis=0).astype(BF16)
    vcat = jnp.concatenate([vbuf_ref[...], vnew_ref[...]], axis=0).astype(BF16)
    mask = valid_ref[...] > 0.5
    for kv in range(n_kv):
        q = q_ref[kv].astype(BF16)
        k = kcat[:, kv * d:(kv + 1) * d]
        v = vcat[:, kv * d:(kv + 1) * d]
        z = lax.dot_general(q, k, (((1,), (1,)), ((), ())), preferred_element_type=F32)
        z = z * (d ** -0.5) + bias_ref[kv]
        z = jnp.where(mask, z, -jnp.inf)
        o_ref[kv] = _sink_softmax_pv(z, sink_ref[kv], v)


def _swa_sample_attention(q, k_new, v_new, k_buf, v_buf, sinks, rel_bias):
    b, t, _ = q.shape
    d, n_kv = HEAD_DIM_B, N_KV_B
    n_heads = q.shape[-1] // d
    group = n_heads // n_kv
    n_buf = k_buf.shape[1]
    pad_t = LANES
    rows = group * t
    bucket, valid = _t5_bucket_table(t, n_buf + pad_t, n_buf)
    valid[:, n_buf + t:] = False
    bias = rel_bias[jnp.asarray(bucket)].astype(F32)
    bias = bias.transpose(2, 0, 1).reshape(n_kv, rows, n_buf + pad_t)
    valid_rows = jnp.asarray(np.tile(valid, (group, 1)).astype(np.float32))
    qr = q.reshape(b, t, n_kv, group, d).transpose(0, 2, 3, 1, 4).reshape(b, n_kv, rows, d)
    knew = jnp.zeros((b, pad_t, n_kv * d), F32).at[:, :t].set(k_new)
    vnew = jnp.zeros((b, pad_t, n_kv * d), F32).at[:, :t].set(v_new)
    sink_rows = jnp.repeat(sinks.astype(F32), t).reshape(n_kv, rows, 1)
    per_b3 = lambda i: (i, 0, 0)
    const3 = lambda i: (0, 0, 0)
    o = pl.pallas_call(
        functools.partial(_swa_sample_kernel, n_kv=n_kv),
        grid=(b,),
        in_specs=[
            pl.BlockSpec((None, n_kv, rows, d), lambda i: (i, 0, 0, 0)),
            pl.BlockSpec((None, n_buf, n_kv * d), per_b3),
            pl.BlockSpec((None, n_buf, n_kv * d), per_b3),
            pl.BlockSpec((None, pad_t, n_kv * d), per_b3),
            pl.BlockSpec((None, pad_t, n_kv * d), per_b3),
            pl.BlockSpec((n_kv, rows, n_buf + pad_t), const3),
            pl.BlockSpec((rows, n_buf + pad_t), lambda i: (0, 0)),
            pl.BlockSpec((n_kv, rows, 1), const3),
        ],
        out_specs=pl.BlockSpec((None, n_kv, rows, d), lambda i: (i, 0, 0, 0)),
        out_shape=jax.ShapeDtypeStruct((b, n_kv, rows, d), F32),
        compiler_params=_params(1), name="swa_sample",
    )(qr, k_buf, v_buf, knew, vnew, bias, valid_rows, sink_rows)
    return o.reshape(b, n_kv, group, t, d).transpose(0, 3, 1, 2, 4).reshape(b, t, n_heads * d)


def _router_kernel(h_ref, w_ref, b_ref, idx_ref, gate_ref, *, n_experts):
    logits = jnp.dot(h_ref[...].astype(BF16), w_ref[...].astype(BF16),
                     preferred_element_type=F32) + b_ref[...]
    lane_i = lax.broadcasted_iota(jnp.int32, logits.shape, 1)
    lane = lane_i.astype(F32)
    neg = -jnp.inf
    lg = jnp.where(lane_i < n_experts, logits, neg)
    m1 = jnp.max(lg, axis=-1, keepdims=True)
    i1 = jnp.min(jnp.where(lg == m1, lane, float(LANES)), axis=-1, keepdims=True)
    lg2 = jnp.where(lane == i1, neg, lg)
    m2 = jnp.max(lg2, axis=-1, keepdims=True)
    i2 = jnp.min(jnp.where(lg2 == m2, lane, float(LANES)), axis=-1, keepdims=True)
    e2 = jnp.exp(m2 - m1)
    den = 1.0 + e2
    idx_ref[...] = jnp.where(lane_i == 0, i1, jnp.where(lane_i == 1, i2, 0.0)).astype(jnp.int32)
    gate_ref[...] = jnp.where(lane_i == 0, 1.0 / den, jnp.where(lane_i == 1, e2 / den, 0.0))


def _router(h, w_router, b_router, *, bm):
    m, k = h.shape
    n_experts = w_router.shape[1]
    w = jnp.zeros((k, LANES), F32).at[:, :n_experts].set(w_router)
    bvec = jnp.zeros((1, LANES), F32).at[0, :n_experts].set(b_router.astype(F32))
    return pl.pallas_call(
        functools.partial(_router_kernel, n_experts=n_experts),
        grid=(m // bm,),
        in_specs=[pl.BlockSpec((bm, k), lambda i: (i, 0)),
                  pl.BlockSpec((k, LANES), lambda i: (0, 0)),
                  pl.BlockSpec((1, LANES), lambda i: (0, 0))],
        out_specs=[pl.BlockSpec((bm, LANES), lambda i: (i, 0))] * 2,
        out_shape=[jax.ShapeDtypeStruct((m, LANES), jnp.int32),
                   jax.ShapeDtypeStruct((m, LANES), F32)],
        compiler_params=_params(1), name="router",
    )(h, w, bvec)


def _row_copy(src_ref, src_row, dst_ref, dst_row, sem):
    return pltpu.make_async_copy(src_ref.at[pl.ds(src_row, 1), :], dst_ref.at[pl.ds(dst_row, 1), :], sem)


def _gather_rows_kernel(idx_ref, src_ref, o_ref, buf_ref, sem, *, rb):
    i = pl.program_id(0)

    def issue(step, slot):
        def start(j, c):
            _row_copy(src_ref, idx_ref[step * rb + j], buf_ref.at[slot], j, sem.at[slot]).start()
            return c

        lax.fori_loop(0, rb, start, 0, unroll=DMA_UNROLL)

    @pl.when(i == 0)
    def _first_block():
        issue(0, 0)

    @pl.when(i + 1 < pl.num_programs(0))
    def _next_block():
        issue(i + 1, (i + 1) & 1)

    slot = i & 1

    def wait(j, c):
        _row_copy(src_ref, 0, buf_ref.at[slot], j, sem.at[slot]).wait()
        return c

    lax.fori_loop(0, rb, wait, 0, unroll=DMA_UNROLL)
    o_ref[...] = buf_ref[slot].astype(o_ref.dtype)


def _gather_rows(src, idx, *, rb, out_dtype):
    s, d = idx.shape[0], src.shape[1]
    assert s % rb == 0
    return pl.pallas_call(
        functools.partial(_gather_rows_kernel, rb=rb),
        grid_spec=pltpu.PrefetchScalarGridSpec(
            num_scalar_prefetch=1, grid=(s // rb,),
            in_specs=[pl.BlockSpec(memory_space=pl.ANY)],
            out_specs=pl.BlockSpec((rb, d), lambda i, idx: (i, 0)),
            scratch_shapes=[pltpu.VMEM((2, rb, d), src.dtype), pltpu.SemaphoreType.DMA((2,))]),
        out_shape=jax.ShapeDtypeStruct((s, d), out_dtype),
        compiler_params=_params(1), name="moe_gather",
    )(idx, src)


def _combine_kernel(dest_ref, yb_ref, gate_ref, o_ref, buf_ref, sem, *, tb):
    base = pl.program_id(0) * tb

    def start(j, c):
        for k in range(TOP_K):
            _row_copy(yb_ref, dest_ref[(base + j) * TOP_K + k], buf_ref.at[k], j, sem).start()
        return c

    def wait(j, c):
        for k in range(TOP_K):
            _row_copy(yb_ref, 0, buf_ref.at[k], j, sem).wait()
        return c

    lax.fori_loop(0, tb, start, 0, unroll=DMA_UNROLL)
    lax.fori_loop(0, tb, wait, 0, unroll=DMA_UNROLL)
    gate = gate_ref[...]
    out = gate[:, 0:1] * buf_ref[0]
    for k in range(1, TOP_K):
        out = out + gate[:, k:k + 1] * buf_ref[k]
    o_ref[...] = out


def _combine(yb, dest, gate, *, tb):
    n, d = gate.shape[0], yb.shape[1]
    assert n % tb == 0
    return pl.pallas_call(
        functools.partial(_combine_kernel, tb=tb),
        grid_spec=pltpu.PrefetchScalarGridSpec(
            num_scalar_prefetch=1, grid=(n // tb,),
            in_specs=[pl.BlockSpec(memory_space=pl.ANY),
                      pl.BlockSpec((tb, LANES), lambda i, dest: (i, 0))],
            out_specs=pl.BlockSpec((tb, d), lambda i, dest: (i, 0)),
            scratch_shapes=[pltpu.VMEM((TOP_K, tb, d), F32), pltpu.SemaphoreType.DMA(())]),
        out_shape=jax.ShapeDtypeStruct((n, d), F32),
        compiler_params=_params(1), name="moe_combine",
    )(dest, yb, gate)


def _moe(h, w_router, b_router, w_gate_e, w_up_e, w_down_e, *, bm_route, bm, bn_up, bn_down, kc_down):
    n, d = h.shape
    n_experts = w_gate_e.shape[0]
    idx, gate = _router(h, w_router, b_router, bm=bm_route)
    flat_e = idx[:, :TOP_K].reshape(-1)
    onehot = (flat_e[:, None] == jnp.arange(n_experts, dtype=jnp.int32)[None, :]).astype(jnp.int32)
    csum = jnp.cumsum(onehot, axis=0)
    rank = jnp.take_along_axis(csum, flat_e[:, None], axis=1)[:, 0] - 1
    counts = csum[-1]
    blocks_e = (counts + bm - 1) // bm
    blocks_end = jnp.cumsum(blocks_e)
    dest = ((blocks_end - blocks_e)[flat_e] * bm + rank).astype(jnp.int32)
    n_blocks = (n * TOP_K) // bm + n_experts
    n_used = blocks_end[-1].astype(jnp.int32)
    blk_ids = jnp.arange(n_blocks, dtype=jnp.int32)
    blk = jnp.minimum(blk_ids, n_used - 1)
    blk_e = jnp.minimum(jnp.searchsorted(blocks_end, blk, side="right"), n_experts - 1).astype(jnp.int32)
    rows_left = counts[blk_e] - (blk - (blocks_end - blocks_e)[blk_e]) * bm
    live = jnp.where(blk_ids < n_used, jnp.where(rows_left > bm // 2, bm, bm // 2), 0).astype(jnp.int32)
    slot_tok = jnp.zeros((n_blocks * bm,), jnp.int32).at[dest].set(
        jnp.repeat(jnp.arange(n, dtype=jnp.int32), TOP_K))
    xg = _gather_rows(h, slot_tok, rb=2 * LANES, out_dtype=BF16)
    act = _gmm(xg, [w_gate_e, w_up_e], blk_e, live, bm=bm, bn=bn_up, kc=d, out_dtype=BF16,
               half_blocks=True, name="moe_up")
    yb = _gmm(act, [w_down_e], blk_e, live, bm=bm, bn=bn_down, kc=kc_down, out_dtype=F32,
              half_blocks=True, name="moe_down")
    return _combine(yb, dest, gate, tb=LANES)


def kernel(x_prompt, x_sample, c_prompt, c_sample, cache_k_a, cache_v_a, page_table, cache_k_b, cache_v_b, g_mix, g_ffn, g_final, w_ada, b_ada, w_qkv_a, w_o_a, sb_bias_a, w_qkv_b, w_o_b, sinks_b, rel_bias, w_gate_d, w_up_d, w_down_d, w_router, b_router, w_gate_e, w_up_e, w_down_e):
    bp, seq, d = x_prompt.shape
    bs, t, _ = x_sample.shape
    n_p, n_s = bp * seq, bs * t
    n = n_p + n_s
    n_heads_a = d // HEAD_DIM_A
    n_heads_b = d // HEAD_DIM_B
    depth = w_ada.shape[0]
    bm = 832 if n % 832 == 0 else 128
    assert n % bm == 0 and n_p % n_s == 0 and n % LANES == 0

    c_rows = 64
    c_all = jnp.zeros((c_rows, d), F32).at[:bp + bs].set(jnp.concatenate([c_prompt, c_sample], axis=0))
    ada = _gmm(jnp.concatenate([c_all] * depth, axis=0), [w_ada], jnp.arange(depth, dtype=jnp.int32),
               jnp.full((depth,), c_rows, jnp.int32), bm=c_rows, bn=512, kc=d, out_dtype=F32,
               pre_silu=True, name="adaln")
    ada = ada.reshape(depth, c_rows, 6, d)[:, :bp + bs] + b_ada.reshape(depth, 1, 6, d)

    def mods(layer, which):
        m = ada[layer, :, which]
        return m[:bp].reshape(bp, 1, d), jnp.repeat(m[bp:], t, axis=0).reshape(1, n_s, d)

    def norm_both(x, g, scale_of, shift_of, f=None, gate_of=None, h_dtype=BF16):
        sc_p, sc_s = mods(*scale_of)
        sh_p, sh_s = mods(*shift_of)
        gt_p, gt_s = mods(*gate_of) if f is not None else (None, None)
        res = f is not None
        xn, h = _norm(x, g, row0=0, batch=bp, rows=seq, out_rows=n, want_x=res, f=f, gate=gt_p,
                      scale=sc_p, shift=sh_p, h_dtype=h_dtype)
        xn_s, h_s = _norm(x, g, row0=n_p, batch=1, rows=n_s, out_rows=n_s, want_x=res, f=f,
                          gate=gt_s, scale=sc_s, shift=sh_s, h_dtype=h_dtype)
        xn = lax.dynamic_update_slice(xn, xn_s, (n_p, 0)) if res else x
        return xn, lax.dynamic_update_slice(h, h_s, (n_p, 0))

    x = jnp.concatenate([x_prompt.reshape(n_p, d), x_sample.reshape(n_s, d)], axis=0)

    _, h = norm_both(x, g_mix[0], (0, 1), (0, 0))
    qkv_a = _dense_mm(h, [w_qkv_a[0]], bm=bm, bn=512, kc=d, out_dtype=F32, name="qkv_a")
    kv_w = N_KV_A * HEAD_DIM_A
    o = _sb_prompt_attention(qkv_a, sb_bias_a[0].astype(F32), batch=bp, seq=seq, n_heads=n_heads_a)
    qkv_s = qkv_a[n_p:].reshape(bs, t, -1)
    k_pool = cache_k_a[0].reshape(cache_k_a.shape[1], PAGE_SIZE * N_KV_A, HEAD_DIM_A)
    v_pool = cache_v_a[0].reshape(cache_v_a.shape[1], PAGE_SIZE * N_KV_A, HEAD_DIM_A)
    o_s = _sb_sample_attention(qkv_s[..., :d], qkv_s[..., d:d + kv_w], qkv_s[..., d + kv_w:],
                               k_pool, v_pool, page_table, sb_bias_a[0])
    o = jnp.concatenate([o, o_s.reshape(n_s, d).astype(BF16)], axis=0)
    f = _dense_mm(o, [w_o_a[0]], bm=bm, bn=512, kc=d, out_dtype=F32, name="wo_a")
    x, h = norm_both(x, g_ffn[0], (0, 4), (0, 3), f=f, gate_of=(0, 2))
    act = _dense_mm(h, [w_gate_d[0], w_up_d[0]], bm=bm // 2, bn=512, kc=d, out_dtype=BF16, name="ffn_up")
    f = _dense_mm(act, [w_down_d[0]], bm=bm // 2, bn=512, kc=w_down_d.shape[1] // 2, out_dtype=F32,
                  name="ffn_down")

    x, h = norm_both(x, g_mix[1], (1, 1), (1, 0), f=f, gate_of=(0, 5))
    qkv_b = _dense_mm(h, [w_qkv_b[0]], bm=bm, bn=512, kc=d, out_dtype=F32, name="qkv_b")
    kvb_w = N_KV_B * HEAD_DIM_B
    bucket, valid = _t5_bucket_table(WINDOW, 2 * WINDOW, WINDOW)
    band_bias = rel_bias[jnp.asarray(bucket)].astype(F32).transpose(2, 0, 1)
    o = _swa_prompt_attention(qkv_b, sinks_b[0].astype(F32), band_bias,
                              jnp.asarray(valid.astype(np.float32)), batch=bp, seq=seq,
                              n_heads=n_heads_b)
    qkvb_s = qkv_b[n_p:].reshape(bs, t, -1)
    n_buf = cache_k_b.shape[2]
    kb_new, vb_new = qkvb_s[..., d:d + kvb_w], qkvb_s[..., d + kvb_w:]
    o_s = _swa_sample_attention(qkvb_s[..., :d], kb_new, vb_new,
                                cache_k_b[0].reshape(bs, n_buf, kvb_w),
                                cache_v_b[0].reshape(bs, n_buf, kvb_w), sinks_b[0], rel_bias)
    o = jnp.concatenate([o, o_s.reshape(n_s, d).astype(BF16)], axis=0)
    f = _dense_mm(o, [w_o_b[0]], bm=bm, bn=512, kc=d, out_dtype=F32, name="wo_b")
    x, h = norm_both(x, g_ffn[1], (1, 4), (1, 3), f=f, gate_of=(1, 2), h_dtype=F32)
    f = _moe(h, w_router[0], b_router[0], w_gate_e[0], w_up_e[0], w_down_e[0], bm_route=bm,
             bm=512, bn_up=512, bn_down=512, kc_down=w_down_e.shape[2] // 2)

    gt_p, gt_s = mods(1, 5)
    _, y_p = _norm(x, g_final, row0=0, batch=bp, rows=seq, out_rows=n_p, f=f, gate=gt_p, h_dtype=F32)
    _, y_s = _norm(x, g_final, row0=n_p, batch=1, rows=n_s, out_rows=n_s, f=f, gate=gt_s, h_dtype=F32)

    pages = seq // PAGE_SIZE
    k_a_prompt = qkv_a[:n_p, d:d + kv_w].reshape(1, bp, pages, PAGE_SIZE, N_KV_A, HEAD_DIM_A)
    v_a_prompt = qkv_a[:n_p, d + kv_w:].reshape(1, bp, pages, PAGE_SIZE, N_KV_A, HEAD_DIM_A)
    k_a_sample = qkv_s[..., d:d + kv_w].reshape(1, bs, t, N_KV_A, HEAD_DIM_A)
    v_a_sample = qkv_s[..., d + kv_w:].reshape(1, bs, t, N_KV_A, HEAD_DIM_A)
    n_keep = min(WINDOW, seq)

    def last_rows(c0):
        return jnp.stack([qkv_b[(i + 1) * seq - n_keep:(i + 1) * seq, c0:c0 + kvb_w] for i in range(bp)]
                         ).reshape(1, bp, n_keep, N_KV_B, HEAD_DIM_B)

    k_b_prompt, v_b_prompt = last_rows(d), last_rows(d + kvb_w)
    k_b_sample = jnp.concatenate([cache_k_b[0][:, t:], kb_new.reshape(bs, t, N_KV_B, HEAD_DIM_B)], axis=1)[None]
    v_b_sample = jnp.concatenate([cache_v_b[0][:, t:], vb_new.reshape(bs, t, N_KV_B, HEAD_DIM_B)], axis=1)[None]
    return (y_p.reshape(bp, seq, d), y_s.reshape(bs, t, d), k_a_prompt, v_a_prompt, k_a_sample,
            v_a_sample, k_b_prompt, v_b_prompt, k_b_sample, v_b_sample)
```

```python
import functools
import math

import numpy as np
import jax
import jax.numpy as jnp
from jax import lax
from jax.experimental import pallas as pl
from jax.experimental.pallas import tpu as pltpu

F32 = jnp.float32
BF16 = jnp.bfloat16

HEAD_DIM_A = 128
N_KV_A = 8
HEAD_DIM_B = 64
N_KV_B = 8
PAGE_SIZE = 128
WINDOW = 128
N_BUCKETS = 32
MAX_DISTANCE = 128
TOP_K = 2
RMS_EPS = 1e-5

LANES = 128
V7X_VMEM_BYTES = 64 * 1024 * 1024
VMEM_LIMIT = V7X_VMEM_BYTES - 8 * 1024 * 1024
CAST_ROWS = 256
DMA_UNROLL = 8


def _params(n_axes):
    return pltpu.CompilerParams(dimension_semantics=("arbitrary",) * n_axes,
                                vmem_limit_bytes=VMEM_LIMIT)


def _sigmoid(x):
    return 1.0 / (1.0 + jnp.exp(-x))


def _gmm_kernel(be_ref, live_ref, nxt_ref, last_ref, a_ref, *refs, n_w, pre_silu, has_acc,
                half_blocks, stream_w, kh):
    w_refs = refs[:n_w]
    acc_ref = refs[n_w] if has_acc else None
    o_ref = refs[n_w + has_acc]
    wb_refs = refs[n_w + has_acc + 1:2 * n_w + has_acc + 1]
    j = pl.program_id(0)
    r = pl.program_id(1)
    e = be_ref[r]
    prev = be_ref[jnp.maximum(r - 1, 0)]
    bm = a_ref.shape[0]
    kc, bn = wb_refs[0].shape
    live = live_ref[r]

    def convert(src):
        def body(i, c):
            rows = pl.ds(pl.multiple_of(i * CAST_ROWS, CAST_ROWS), CAST_ROWS)
            for w, wb_ref in enumerate(wb_refs):
                wb_ref[rows, :] = src(w, rows).astype(BF16)
            return c

        lax.fori_loop(0, kc // CAST_ROWS, body, 0)

    if stream_w:
        stage_refs = refs[2 * n_w + has_acc + 1:3 * n_w + has_acc + 1]
        sem = refs[3 * n_w + has_acc + 1]

        def w_copy(w, group, col):
            cols = pl.ds(pl.multiple_of(col * bn, bn), bn)
            return pltpu.make_async_copy(w_refs[w].at[group, pl.ds(kh * kc, kc), cols],
                                         stage_refs[w], sem.at[w])

        @pl.when(jnp.logical_or(r == 0, e != prev))
        def _switch_weights():
            @pl.when(jnp.logical_and(j == 0, r == 0))
            def _first_tile():
                for w in range(n_w):
                    w_copy(w, e, j).start()

            for w in range(n_w):
                w_copy(w, e, j).wait()
            convert(lambda w, rows: stage_refs[w][rows, :])
            is_last = last_ref[r]

            @pl.when(jnp.logical_not(jnp.logical_and(is_last == 1, j == pl.num_programs(0) - 1)))
            def _prefetch_next():
                for w in range(n_w):
                    w_copy(w, nxt_ref[r], j + is_last).start()
    else:
        @pl.when(jnp.logical_or(r == 0, e != prev))
        def _convert_weights():
            convert(lambda w, rows: w_refs[w][0, rows, :])

    def passthrough(rows):
        o_ref[rows, :] = acc_ref[rows, :] if has_acc else jnp.zeros_like(o_ref[rows, :])

    def compute(n_rows):
        rows = slice(0, n_rows)
        a = a_ref[rows, :]
        if pre_silu:
            a = a.astype(F32)
            a = a * _sigmoid(a)
        a = a.astype(BF16)
        if n_w == 2:
            g = jnp.dot(a, wb_refs[0][...], preferred_element_type=F32)
            u = jnp.dot(a, wb_refs[1][...], preferred_element_type=F32)
            res = g * _sigmoid(g) * u
        else:
            res = jnp.dot(a, wb_refs[0][...], preferred_element_type=F32)
        if has_acc:
            res = acc_ref[rows, :] + res
        o_ref[rows, :] = res.astype(o_ref.dtype)
        if n_rows < bm:
            passthrough(slice(n_rows, bm))

    if half_blocks:
        @pl.when(live > bm // 2)
        def _full_block():
            compute(bm)

        @pl.when(jnp.logical_and(live > 0, live <= bm // 2))
        def _half_block():
            compute(bm // 2)
    else:
        @pl.when(live > 0)
        def _block():
            compute(bm)

    @pl.when(live == 0)
    def _unused_block():
        passthrough(slice(0, bm))


def _gmm(a, ws, blk_e, live, *, bm, bn, kc, out_dtype, pre_silu=False, half_blocks=False,
         stream_w=False, name="gmm"):
    m, k = a.shape
    _, _, n = ws[0].shape
    assert m % bm == 0 and n % bn == 0 and k % kc == 0 and kc % CAST_ROWS == 0
    n_w = len(ws)
    nb = m // bm
    ids = jnp.arange(nb, dtype=jnp.int32)
    starts = jnp.where((ids == 0) | (blk_e != jnp.roll(blk_e, 1)), ids, nb)
    nxt_start = jnp.concatenate([lax.cummin(starts, reverse=True)[1:], jnp.full((1,), nb, jnp.int32)])
    last = (nxt_start == nb).astype(jnp.int32)
    nxt_e = blk_e[jnp.where(nxt_start == nb, 0, nxt_start)]
    out = None
    for kh in range(k // kc):
        has_acc = out is not None
        in_specs = [pl.BlockSpec((bm, kc), lambda j, r, be, lv, nx, ls, kh=kh: (r, kh))]
        if stream_w:
            in_specs += [pl.BlockSpec(memory_space=pl.ANY) for _ in ws]
        else:
            in_specs += [pl.BlockSpec((1, kc, bn), lambda j, r, be, lv, nx, ls, kh=kh: (be[r], kh, j))
                         for _ in ws]
        io_spec = pl.BlockSpec((bm, bn), lambda j, r, be, lv, nx, ls: (r, j))
        args = [blk_e, live, nxt_e, last, a, *ws]
        if has_acc:
            in_specs.append(io_spec)
            args.append(out)
        scratch = [pltpu.VMEM((kc, bn), BF16) for _ in ws]
        if stream_w:
            scratch += [pltpu.VMEM((kc, bn), F32) for _ in ws] + [pltpu.SemaphoreType.DMA((n_w,))]
        out = pl.pallas_call(
            functools.partial(_gmm_kernel, n_w=n_w, pre_silu=pre_silu, has_acc=has_acc,
                              half_blocks=half_blocks, stream_w=stream_w, kh=kh),
            grid_spec=pltpu.PrefetchScalarGridSpec(
                num_scalar_prefetch=4, grid=(n // bn, nb), in_specs=in_specs,
                out_specs=io_spec, scratch_shapes=scratch),
            out_shape=jax.ShapeDtypeStruct((m, n), out_dtype),
            input_output_aliases={5 + n_w: 0} if has_acc else {},
            compiler_params=_params(2), name=name,
        )(*args)
    return out


def _dense_mm(a, ws, *, bm, bn, kc, out_dtype, pre_silu=False, name="mm"):
    nb = a.shape[0] // bm
    return _gmm(a, [w[None] for w in ws], jnp.zeros((nb,), jnp.int32),
                jnp.full((nb,), bm, jnp.int32), bm=bm, bn=bn, kc=kc, out_dtype=out_dtype,
                pre_silu=pre_silu, name=name)


def _norm_kernel(*refs, has_res, has_mod, want_x):
    it = iter(refs)
    x = next(it)[...]
    if has_res:
        f_ref = next(it)
        gate_ref = next(it)
        x = x + gate_ref[...] * f_ref[...]
    g_ref = next(it)
    if has_mod:
        sc_ref = next(it)
        sh_ref = next(it)
    if want_x:
        next(it)[...] = x
    h_ref = next(it)
    y = x * lax.rsqrt(jnp.mean(x * x, axis=-1, keepdims=True) + RMS_EPS) * g_ref[...]
    if has_mod:
        y = y * (1.0 + sc_ref[...]) + sh_ref[...]
    h_ref[...] = y.astype(h_ref.dtype)


def _norm(x, g, *, row0, batch, rows, out_rows, want_x=False, f=None, gate=None, scale=None,
          shift=None, h_dtype=BF16, ts=LANES, name="norm"):
    n, d = x.shape
    ts = min(ts, rows)
    assert rows % ts == 0 and row0 % ts == 0 and n % ts == 0
    nt = rows // ts
    has_res, has_mod = f is not None, scale is not None
    assert has_res or not want_x
    in_blk0 = row0 // ts
    full = out_rows == n
    out_blk0 = in_blk0 if full else 0
    n_blocks = (n - row0) // ts if full else batch * nt
    in_spec = pl.BlockSpec((ts, d), lambda i: (in_blk0 + i, 0))
    out_spec = pl.BlockSpec((ts, d), lambda i: (out_blk0 + i, 0))

    def mod_spec(v):
        if v.shape[1] == 1:
            return pl.BlockSpec((None, 1, d), lambda i: (jnp.minimum(i // nt, batch - 1), 0, 0))
        assert batch == 1
        return pl.BlockSpec((None, ts, d), lambda i: (0, i, 0))

    args, specs = [x], [in_spec]
    if has_res:
        args += [f, gate]
        specs += [in_spec, mod_spec(gate)]
    args.append(g.reshape(1, d))
    specs.append(pl.BlockSpec((1, d), lambda i: (0, 0)))
    if has_mod:
        args += [scale, shift]
        specs += [mod_spec(scale), mod_spec(shift)]
    out_shape, out_specs = [], []
    if want_x:
        out_shape.append(jax.ShapeDtypeStruct((out_rows, d), F32))
        out_specs.append(out_spec)
    out_shape.append(jax.ShapeDtypeStruct((out_rows, d), h_dtype))
    out_specs.append(out_spec)
    outs = pl.pallas_call(
        functools.partial(_norm_kernel, has_res=has_res, has_mod=has_mod, want_x=want_x),
        grid=(n_blocks,), in_specs=specs, out_specs=out_specs, out_shape=out_shape,
        compiler_params=_params(1), name=name,
    )(*args)
    return (outs[0], outs[1]) if want_x else (None, outs[0])


def _neg_softplus(z):
    return -(jnp.maximum(z, 0.0) + jnp.log(1.0 + jnp.exp(-jnp.abs(z))))


def _split_bf16(l):
    hi = l.astype(BF16)
    mid = (l - hi.astype(F32)).astype(BF16)
    return hi, mid


def _suffix_sum_matrix(blk):
    j = np.arange(blk)
    strict = (j[:, None] > j[None, :]).astype(np.float32)
    half = np.concatenate([strict, np.ones((blk, blk), np.float32)], axis=1)
    return jnp.asarray(np.concatenate([half, half], axis=0), BF16)


def _newer_key_matrix(blk):
    s = np.arange(blk)
    u = (s[None, :] > s[:, None]).astype(np.float32)
    return jnp.asarray(np.concatenate([u, u], axis=1), BF16)


def _sb_prompt_kernel(sb_ref, q_ref, k_ref, v_ref, umat_ref, o_ref, kb_ref, vb_ref, qb_ref,
                      bias_ref, carry_ref, acc_ref, *, group, blk):
    kv = pl.program_id(1)
    i = pl.program_id(2)
    d = HEAD_DIM_A

    @pl.when(i == 0)
    def _per_kv_head_setup():
        kb_ref[...] = k_ref[...].astype(BF16)
        vb_ref[...] = v_ref[...].astype(BF16)
        for g in range(group):
            bias_ref[g * blk:(g + 1) * blk, :] = jnp.full((blk, 2 * blk), sb_ref[kv * group + g], F32)

    for g in range(group):
        qb_ref[g * blk:(g + 1) * blk, :] = q_ref[:, g * d:(g + 1) * d].astype(BF16)
    rows = lax.broadcasted_iota(jnp.int32, (group * blk, blk), 0)
    cols = lax.broadcasted_iota(jnp.int32, (group * blk, blk), 1)
    causal = cols < (rows & (blk - 1))
    nt_dims = (((1,), (1,)), ((), ()))

    def block(off, mask, first):
        sl = pl.ds(pl.multiple_of(off, blk), blk)
        z = lax.dot_general(qb_ref[...], kb_ref[sl, :], nt_dims, preferred_element_type=F32)
        z = z * (d ** -0.5) + bias_ref[:, :blk]
        l = _neg_softplus(z)
        if mask is not None:
            l = jnp.where(mask, l, 0.0)
        hi, mid = _split_bf16(l)
        bt = jnp.dot(jnp.concatenate([hi, mid], axis=1), umat_ref[...], preferred_element_type=F32)
        logit = z + l + bt[:, :blk]
        if not first:
            logit = logit + carry_ref[...]
        a = jnp.exp(logit)
        if mask is not None:
            a = jnp.where(mask, a, 0.0)
        pv = jnp.dot(a.astype(BF16), vb_ref[sl, :], preferred_element_type=F32)
        if first:
            acc_ref[...] = pv
            carry_ref[...] = bt[:, blk:]
        else:
            acc_ref[...] += pv
            carry_ref[...] += bt[:, blk:]

    def pair_scores(off):
        sl = pl.ds(pl.multiple_of(off, blk), 2 * blk)
        z = lax.dot_general(qb_ref[...], kb_ref[sl, :], nt_dims, preferred_element_type=F32)
        z = z * (d ** -0.5) + bias_ref[...]
        l = _neg_softplus(z)
        hi, mid = _split_bf16(l)
        umat = umat_ref[...]
        bt_old = jnp.dot(jnp.concatenate([hi[:, :blk], mid[:, :blk]], axis=1), umat,
                         preferred_element_type=F32)
        bt_new = jnp.dot(jnp.concatenate([hi[:, blk:], mid[:, blk:]], axis=1), umat,
                         preferred_element_type=F32)
        return sl, z + l, bt_old, bt_new

    def pair_apply(scores, carry):
        sl, zl, bt_old, bt_new = scores
        newer = jnp.concatenate([bt_old[:, :blk] + bt_new[:, blk:] + carry, bt_new[:, :blk] + carry],
                                axis=1)
        pv = jnp.dot(jnp.exp(zl + newer).astype(BF16), vb_ref[sl, :], preferred_element_type=F32)
        return pv, carry + bt_new[:, blk:] + bt_old[:, blk:]

    def block_pairs(offs):
        scores = [pair_scores(off) for off in offs]
        carry = carry_ref[...]
        acc = acc_ref[...]
        for s in scores:
            pv, carry = pair_apply(s, carry)
            acc = acc + pv
        acc_ref[...] = acc
        carry_ref[...] = carry

    block(i * blk, causal, True)

    def older_quad(j, c):
        block_pairs([(i - 2 - 4 * j) * blk, (i - 4 - 4 * j) * blk])
        return c

    lax.fori_loop(0, lax.shift_right_logical(i, 2), older_quad, 0)
    rest = i & 3

    @pl.when((rest & 2) == 2)
    def _oldest_pair():
        block_pairs([(rest & 1) * blk])

    @pl.when((rest & 1) == 1)
    def _oldest_block():
        block(0, None, False)

    for g in range(group):
        o_ref[:, g * d:(g + 1) * d] = acc_ref[g * blk:(g + 1) * blk, :].astype(o_ref.dtype)


def _sb_prompt_attention(qkv, sb_bias, *, batch, seq, n_heads):
    d, blk = HEAD_DIM_A, PAGE_SIZE
    assert blk & (blk - 1) == 0
    group = n_heads // N_KV_A
    nq = seq // blk
    kernel = functools.partial(_sb_prompt_kernel, group=group, blk=blk)
    return pl.pallas_call(
        kernel,
        grid_spec=pltpu.PrefetchScalarGridSpec(
            num_scalar_prefetch=1, grid=(batch, N_KV_A, nq),
            in_specs=[
                pl.BlockSpec((blk, group * d), lambda b, kv, i, sb: (b * nq + i, kv)),
                pl.BlockSpec((seq, d), lambda b, kv, i, sb: (b, n_heads + kv)),
                pl.BlockSpec((seq, d), lambda b, kv, i, sb: (b, n_heads + N_KV_A + kv)),
                pl.BlockSpec((2 * blk, 2 * blk), lambda b, kv, i, sb: (0, 0)),
            ],
            out_specs=pl.BlockSpec((blk, group * d), lambda b, kv, i, sb: (b * nq + i, kv)),
            scratch_shapes=[pltpu.VMEM((seq, d), BF16), pltpu.VMEM((seq, d), BF16),
                            pltpu.VMEM((group * blk, d), BF16), pltpu.VMEM((group * blk, 2 * blk), F32),
                            pltpu.VMEM((group * blk, blk), F32), pltpu.VMEM((group * blk, d), F32)]),
        out_shape=jax.ShapeDtypeStruct((batch * seq, n_heads * d), BF16),
        compiler_params=_params(3), name="sb_prompt",
    )(sb_bias, qkv, qkv, qkv, _suffix_sum_matrix(blk))


def _sb_sample_kernel(pt_ref, qbd_ref, bias_ref, knew_ref, vnew_ref, *refs, n_kv, rows_per_kv,
                      t_new, n_pg):
    k_refs, v_refs = refs[:n_pg], refs[n_pg:2 * n_pg]
    u_ref, o_ref, carry_ref, acc_ref, logw_ref = refs[2 * n_pg:]
    p = pl.program_id(1)
    d, blk = HEAD_DIM_A, PAGE_SIZE
    qbd = qbd_ref[...]
    c = qbd.shape[1]

    def log_weights(kcat, mask, carry):
        zt = jnp.dot(kcat, qbd, preferred_element_type=F32) * (d ** -0.5) + bias_ref[...]
        l = _neg_softplus(zt)
        if mask is not None:
            l = jnp.where(mask, l, 0.0)
        hi, mid = _split_bf16(l)
        newer = []
        for j in range(kcat.shape[0] // blk):
            rows = slice(j * blk, (j + 1) * blk)
            between = jnp.dot(u_ref[...], jnp.concatenate([hi[rows], mid[rows]], axis=0),
                              preferred_element_type=F32)
            newer.append(between + carry)
            carry = carry + jnp.sum(l[rows], axis=0, keepdims=True)
        return zt + l + jnp.concatenate(newer, axis=0), carry

    def weighted_values(logw, v_of_kv, mask):
        a = jnp.exp(logw)
        if mask is not None:
            a = jnp.where(mask, a, 0.0)
        at = a.T.astype(BF16)
        return jnp.concatenate(
            [jnp.dot(at[kv * rows_per_kv:(kv + 1) * rows_per_kv, :], v_of_kv(kv),
                     preferred_element_type=F32) for kv in range(n_kv)], axis=0)

    @pl.when(p == 0)
    def _new_tokens():
        pad = jnp.zeros((blk - knew_ref.shape[0], knew_ref.shape[1]), F32)
        kn = jnp.concatenate([knew_ref[...], pad], axis=0).astype(BF16)
        vn = jnp.concatenate([vnew_ref[...], pad], axis=0).astype(BF16)
        s_idx = lax.broadcasted_iota(jnp.int32, (blk, c), 0)
        t_idx = lax.broadcasted_iota(jnp.int32, (blk, c), 1) % t_new
        mask = s_idx < t_idx
        logw, carry = log_weights(kn, mask, jnp.zeros((1, c), F32))
        acc_ref[...] = weighted_values(logw, lambda kv: vn[:, kv * d:(kv + 1) * d], mask)
        carry_ref[...] = jnp.broadcast_to(carry, carry_ref.shape)
        logw_ref[...] = jnp.full(logw_ref.shape, -jnp.inf, F32)

    def kv_rows(page_refs, kv):
        return jnp.concatenate([r[pl.ds(kv, blk, stride=n_kv), :].astype(BF16) for r in page_refs],
                               axis=0)

    out = weighted_values(logw_ref[...], lambda kv: kv_rows(v_refs, kv), None)
    kcat = jnp.concatenate([kv_rows(k_refs, kv) for kv in range(n_kv)], axis=1)
    logw, carry = log_weights(kcat, None, carry_ref[0:1, :])
    acc_ref[...] += out
    logw_ref[...] = logw
    carry_ref[...] = jnp.broadcast_to(carry, carry_ref.shape)

    @pl.when(p == pl.num_programs(1) - 1)
    def _finish():
        o_ref[...] = acc_ref[...]


def _sb_sample_attention(q, k_new, v_new, k_pool, v_pool, page_table, sb_bias):
    b, t, _ = q.shape
    d, n_kv = HEAD_DIM_A, N_KV_A
    n_heads = q.shape[-1] // d
    group = n_heads // n_kv
    rows_per_kv = group * t
    c = n_kv * rows_per_kv
    n_pages = page_table.shape[1]
    n_pg = 4 if n_pages % 4 == 0 else (2 if n_pages % 2 == 0 else 1)
    q5 = q.reshape(b, t, n_kv, group, d).transpose(0, 2, 4, 3, 1).reshape(b, n_kv, d, rows_per_kv)
    qbd = jnp.einsum("bkdc,kj->bkdjc", q5, jnp.eye(n_kv, dtype=q.dtype))
    qbd = qbd.reshape(b, n_kv * d, c).astype(BF16)
    bias = jnp.repeat(sb_bias.astype(F32), t).reshape(1, c)
    pad_t = 8
    knew = jnp.zeros((b, pad_t, n_kv * d), F32).at[:, :t].set(k_new)
    vnew = jnp.zeros((b, pad_t, n_kv * d), F32).at[:, :t].set(v_new)
    kernel = functools.partial(_sb_sample_kernel, n_kv=n_kv, rows_per_kv=rows_per_kv, t_new=t,
                               n_pg=n_pg)

    n_steps = n_pages // n_pg

    def page(j, lag):
        def index(i, p, pt):
            step = jnp.clip(p - lag, 0, n_steps - 1)
            return pt[i, n_pages - 1 - (step * n_pg + j)], 0, 0
        return index

    per_b = lambda i, p, pt: (i, 0, 0)
    page_specs = lambda lag: [pl.BlockSpec((None, PAGE_SIZE * n_kv, d), page(j, lag))
                              for j in range(n_pg)]
    o = pl.pallas_call(
        kernel,
        grid_spec=pltpu.PrefetchScalarGridSpec(
            num_scalar_prefetch=1, grid=(b, n_steps + 1),
            in_specs=[
                pl.BlockSpec((None, n_kv * d, c), per_b),
                pl.BlockSpec((1, c), lambda i, p, pt: (0, 0)),
                pl.BlockSpec((None, pad_t, n_kv * d), per_b),
                pl.BlockSpec((None, pad_t, n_kv * d), per_b),
                *page_specs(0), *page_specs(1),
                pl.BlockSpec((PAGE_SIZE, 2 * PAGE_SIZE), lambda i, p, pt: (0, 0)),
            ],
            out_specs=pl.BlockSpec((None, c, d), per_b),
            scratch_shapes=[pltpu.VMEM((8, c), F32), pltpu.VMEM((c, d), F32),
                            pltpu.VMEM((n_pg * PAGE_SIZE, c), F32)]),
        out_shape=jax.ShapeDtypeStruct((b, c, d), F32),
        compiler_params=_params(2), name="sb_sample",
    )(page_table, qbd, bias, knew, vnew, *([k_pool] * n_pg), *([v_pool] * n_pg),
      _newer_key_matrix(PAGE_SIZE))
    return o.reshape(b, n_kv, group, t, d).transpose(0, 3, 1, 2, 4).reshape(b, t, n_heads * d)


def _t5_bucket_table(n_q, n_k, offset):
    dist = offset + np.arange(n_q)[:, None] - np.arange(n_k)[None, :]
    valid = (dist >= 0) & (dist < WINDOW)
    dd = np.maximum(dist, 0)
    max_exact = N_BUCKETS // 2
    far = max_exact + (np.log(np.maximum(dd, 1).astype(np.float32) / np.float32(max_exact))
                       / np.float32(math.log(MAX_DISTANCE / max_exact))
                       * np.float32(N_BUCKETS - max_exact)).astype(np.int32)
    bucket = np.where(dd < max_exact, dd, np.minimum(far, N_BUCKETS - 1))
    return bucket.astype(np.int32), valid


def _sink_softmax_pv(z, sink, v):
    m = jnp.maximum(jnp.max(z, axis=-1, keepdims=True), sink)
    e = jnp.exp(z - m)
    p = e / (jnp.sum(e, axis=-1, keepdims=True) + jnp.exp(sink - m))
    return jnp.dot(p.astype(BF16), v, preferred_element_type=F32)


def _swa_prompt_kernel(sink_ref, q_ref, kprev_ref, kcur_ref, vprev_ref, vcur_ref, bias_ref,
                       valid_ref, o_ref, *, heads_per_step, group):
    n = pl.program_id(1)
    hp = pl.program_id(2)
    d = HEAD_DIM_B
    w = kcur_ref.shape[0]
    col = lax.broadcasted_iota(jnp.int32, (w, 2 * w), 1)
    mask = jnp.logical_and(valid_ref[...] > 0.5, col >= jnp.where(n > 0, 0, w))
    kcat = jnp.concatenate([kprev_ref[...], kcur_ref[...]], axis=0).astype(BF16)
    vcat = jnp.concatenate([vprev_ref[...], vcur_ref[...]], axis=0).astype(BF16)
    for kvl in range(heads_per_step // group):
        k = kcat[:, kvl * d:(kvl + 1) * d]
        v = vcat[:, kvl * d:(kvl + 1) * d]
        for g in range(group):
            hl = kvl * group + g
            q = q_ref[:, hl * d:(hl + 1) * d].astype(BF16)
            z = lax.dot_general(q, k, (((1,), (1,)), ((), ())), preferred_element_type=F32)
            z = z * (d ** -0.5) + bias_ref[hl]
            z = jnp.where(mask, z, -jnp.inf)
            sink = sink_ref[hp * heads_per_step + hl]
            o_ref[:, hl * d:(hl + 1) * d] = _sink_softmax_pv(z, sink, v).astype(o_ref.dtype)


def _swa_prompt_attention(qkv, sinks, bias, valid, *, batch, seq, n_heads):
    d, w = HEAD_DIM_B, WINDOW
    group = n_heads // N_KV_B
    kv_per_step = LANES // d
    hps = kv_per_step * group
    nb = seq // w
    n_steps = n_heads // hps
    q_blocks = n_heads * d // LANES
    k_blocks = N_KV_B * d // LANES
    cur = lambda off: (lambda b, n, hp, s: (b * nb + n, off + hp))
    prev = lambda off: (lambda b, n, hp, s: (b * nb + jnp.maximum(n - 1, 0), off + hp))
    kernel = functools.partial(_swa_prompt_kernel, heads_per_step=hps, group=group)
    return pl.pallas_call(
        kernel,
        grid_spec=pltpu.PrefetchScalarGridSpec(
            num_scalar_prefetch=1, grid=(batch, nb, n_steps),
            in_specs=[
                pl.BlockSpec((w, hps * d), lambda b, n, hp, s: (b * nb + n, hp)),
                pl.BlockSpec((w, LANES), prev(q_blocks)),
                pl.BlockSpec((w, LANES), cur(q_blocks)),
                pl.BlockSpec((w, LANES), prev(q_blocks + k_blocks)),
                pl.BlockSpec((w, LANES), cur(q_blocks + k_blocks)),
                pl.BlockSpec((hps, w, 2 * w), lambda b, n, hp, s: (hp, 0, 0)),
                pl.BlockSpec((w, 2 * w), lambda b, n, hp, s: (0, 0)),
            ],
            out_specs=pl.BlockSpec((w, hps * d), lambda b, n, hp, s: (b * nb + n, hp))),
        out_shape=jax.ShapeDtypeStruct((batch * seq, n_heads * d), BF16),
        compiler_params=_params(3), name="swa_prompt",
    )(sinks, qkv, qkv, qkv, qkv, qkv, bias, valid)


def _swa_sample_kernel(q_ref, kbuf_ref, vbuf_ref, knew_ref, vnew_ref, bias_ref, valid_ref, sink_ref,
                       o_ref, *, n_kv):
    d = HEAD_DIM_B
    kcat = jnp.concatenate([kbuf_ref[...], knew_ref[...]], axis=0).astype(BF16)
    vcat = jnp.concatenate([vbuf_ref[...], vnew_ref[...]], axis=0).astype(BF16)
    mask = valid_ref[...] > 0.5
    for kv in range(n_kv):
        q = q_ref[kv].astype(BF16)
        k = kcat[:, kv * d:(kv + 1) * d]
        v = vcat[:, kv * d:(kv + 1) * d]
        z = lax.dot_general(q, k, (((1,), (1,)), ((), ())), preferred_element_type=F32)
        z = z * (d ** -0.5) + bias_ref[kv]
        z = jnp.where(mask, z, -jnp.inf)
        o_ref[kv] = _sink_softmax_pv(z, sink_ref[kv], v)


def _swa_sample_attention(q, k_new, v_new, k_buf, v_buf, sinks, rel_bias):
    b, t, _ = q.shape
    d, n_kv = HEAD_DIM_B, N_KV_B
    n_heads = q.shape[-1] // d
    group = n_heads // n_kv
    n_buf = k_buf.shape[1]
    pad_t = LANES
    rows = group * t
    bucket, valid = _t5_bucket_table(t, n_buf + pad_t, n_buf)
    valid[:, n_buf + t:] = False
    bias = rel_bias[jnp.asarray(bucket)].astype(F32)
    bias = bias.transpose(2, 0, 1).reshape(n_kv, rows, n_buf + pad_t)
    valid_rows = jnp.asarray(np.tile(valid, (group, 1)).astype(np.float32))
    qr = q.reshape(b, t, n_kv, group, d).transpose(0, 2, 3, 1, 4).reshape(b, n_kv, rows, d)
    knew = jnp.zeros((b, pad_t, n_kv * d), F32).at[:, :t].set(k_new)
    vnew = jnp.zeros((b, pad_t, n_kv * d), F32).at[:, :t].set(v_new)
    sink_rows = jnp.repeat(sinks.astype(F32), t).reshape(n_kv, rows, 1)
    per_b3 = lambda i: (i, 0, 0)
    const3 = lambda i: (0, 0, 0)
    o = pl.pallas_call(
        functools.partial(_swa_sample_kernel, n_kv=n_kv),
        grid=(b,),
        in_specs=[
            pl.BlockSpec((None, n_kv, rows, d), lambda i: (i, 0, 0, 0)),
            pl.BlockSpec((None, n_buf, n_kv * d), per_b3),
            pl.BlockSpec((None, n_buf, n_kv * d), per_b3),
            pl.BlockSpec((None, pad_t, n_kv * d), per_b3),
            pl.BlockSpec((None, pad_t, n_kv * d), per_b3),
            pl.BlockSpec((n_kv, rows, n_buf + pad_t), const3),
            pl.BlockSpec((rows, n_buf + pad_t), lambda i: (0, 0)),
            pl.BlockSpec((n_kv, rows, 1), const3),
        ],
        out_specs=pl.BlockSpec((None, n_kv, rows, d), lambda i: (i, 0, 0, 0)),
        out_shape=jax.ShapeDtypeStruct((b, n_kv, rows, d), F32),
        compiler_params=_params(1), name="swa_sample",
    )(qr, k_buf, v_buf, knew, vnew, bias, valid_rows, sink_rows)
    return o.reshape(b, n_kv, group, t, d).transpose(0, 3, 1, 2, 4).reshape(b, t, n_heads * d)


def _router_kernel(h_ref, w_ref, b_ref, idx_ref, gate_ref, *, n_experts):
    logits = jnp.dot(h_ref[...].astype(BF16), w_ref[...].astype(BF16),
                     preferred_element_type=F32) + b_ref[...]
    lane_i = lax.broadcasted_iota(jnp.int32, logits.shape, 1)
    lane = lane_i.astype(F32)
    neg = -jnp.inf
    lg = jnp.where(lane_i < n_experts, logits, neg)
    m1 = jnp.max(lg, axis=-1, keepdims=True)
    i1 = jnp.min(jnp.where(lg == m1, lane, float(LANES)), axis=-1, keepdims=True)
    lg2 = jnp.where(lane == i1, neg, lg)
    m2 = jnp.max(lg2, axis=-1, keepdims=True)
    i2 = jnp.min(jnp.where(lg2 == m2, lane, float(LANES)), axis=-1, keepdims=True)
    e2 = jnp.exp(m2 - m1)
    den = 1.0 + e2
    idx_ref[...] = jnp.where(lane_i == 0, i1, jnp.where(lane_i == 1, i2, 0.0)).astype(jnp.int32)
    gate_ref[...] = jnp.where(lane_i == 0, 1.0 / den, jnp.where(lane_i == 1, e2 / den, 0.0))


def _router(h, w_router, b_router, *, bm):
    m, k = h.shape
    n_experts = w_router.shape[1]
    w = jnp.zeros((k, LANES), F32).at[:, :n_experts].set(w_router)
    bvec = jnp.zeros((1, LANES), F32).at[0, :n_experts].set(b_router.astype(F32))
    return pl.pallas_call(
        functools.partial(_router_kernel, n_experts=n_experts),
        grid=(m // bm,),
        in_specs=[pl.BlockSpec((bm, k), lambda i: (i, 0)),
                  pl.BlockSpec((k, LANES), lambda i: (0, 0)),
                  pl.BlockSpec((1, LANES), lambda i: (0, 0))],
        out_specs=[pl.BlockSpec((bm, LANES), lambda i: (i, 0))] * 2,
        out_shape=[jax.ShapeDtypeStruct((m, LANES), jnp.int32),
                   jax.ShapeDtypeStruct((m, LANES), F32)],
        compiler_params=_params(1), name="router",
    )(h, w, bvec)


def _row_copy(src_ref, src_row, dst_ref, dst_row, sem):
    return pltpu.make_async_copy(src_ref.at[pl.ds(src_row, 1), :], dst_ref.at[pl.ds(dst_row, 1), :], sem)


def _gather_rows_kernel(idx_ref, src_ref, o_ref, buf_ref, sem, *, rb):
    i = pl.program_id(0)

    def issue(step, slot):
        def start(j, c):
            _row_copy(src_ref, idx_ref[step * rb + j], buf_ref.at[slot], j, sem.at[slot]).start()
            return c

        lax.fori_loop(0, rb, start, 0, unroll=DMA_UNROLL)

    @pl.when(i == 0)
    def _first_block():
        issue(0, 0)

    @pl.when(i + 1 < pl.num_programs(0))
    def _next_block():
        issue(i + 1, (i + 1) & 1)

    slot = i & 1

    def wait(j, c):
        _row_copy(src_ref, 0, buf_ref.at[slot], j, sem.at[slot]).wait()
        return c

    lax.fori_loop(0, rb, wait, 0, unroll=DMA_UNROLL)
    o_ref[...] = buf_ref[slot].astype(o_ref.dtype)


def _gather_rows(src, idx, *, rb, out_dtype):
    s, d = idx.shape[0], src.shape[1]
    assert s % rb == 0
    return pl.pallas_call(
        functools.partial(_gather_rows_kernel, rb=rb),
        grid_spec=pltpu.PrefetchScalarGridSpec(
            num_scalar_prefetch=1, grid=(s // rb,),
            in_specs=[pl.BlockSpec(memory_space=pl.ANY)],
            out_specs=pl.BlockSpec((rb, d), lambda i, idx: (i, 0)),
            scratch_shapes=[pltpu.VMEM((2, rb, d), src.dtype), pltpu.SemaphoreType.DMA((2,))]),
        out_shape=jax.ShapeDtypeStruct((s, d), out_dtype),
        compiler_params=_params(1), name="moe_gather",
    )(idx, src)


def _combine_kernel(dest_ref, yb_ref, gate_ref, o_ref, buf_ref, sem, *, tb):
    base = pl.program_id(0) * tb

    def start(j, c):
        for k in range(TOP_K):
            _row_copy(yb_ref, dest_ref[(base + j) * TOP_K + k], buf_ref.at[k], j, sem).start()
        return c

    def wait(j, c):
        for k in range(TOP_K):
            _row_copy(yb_ref, 0, buf_ref.at[k], j, sem).wait()
        return c

    lax.fori_loop(0, tb, start, 0, unroll=DMA_UNROLL)
    lax.fori_loop(0, tb, wait, 0, unroll=DMA_UNROLL)
    gate = gate_ref[...]
    out = gate[:, 0:1] * buf_ref[0]
    for k in range(1, TOP_K):
        out = out + gate[:, k:k + 1] * buf_ref[k]
    o_ref[...] = out


def _combine(yb, dest, gate, *, tb):
    n, d = gate.shape[0], yb.shape[1]
    assert n % tb == 0
    return pl.pallas_call(
        functools.partial(_combine_kernel, tb=tb),
        grid_spec=pltpu.PrefetchScalarGridSpec(
            num_scalar_prefetch=1, grid=(n // tb,),
            in_specs=[pl.BlockSpec(memory_space=pl.ANY),
                      pl.BlockSpec((tb, LANES), lambda i, dest: (i, 0))],
            out_specs=pl.BlockSpec((tb, d), lambda i, dest: (i, 0)),
            scratch_shapes=[pltpu.VMEM((TOP_K, tb, d), F32), pltpu.SemaphoreType.DMA(())]),
        out_shape=jax.ShapeDtypeStruct((n, d), F32),
        compiler_params=_params(1), name="moe_combine",
    )(dest, yb, gate)


def _moe(h, w_router, b_router, w_gate_e, w_up_e, w_down_e, *, bm_route, bm, bn_up, bn_down, kc_down):
    n, d = h.shape
    n_experts = w_gate_e.shape[0]
    idx, gate = _router(h, w_router, b_router, bm=bm_route)
    flat_e = idx[:, :TOP_K].reshape(-1)
    onehot = (flat_e[:, None] == jnp.arange(n_experts, dtype=jnp.int32)[None, :]).astype(jnp.int32)
    csum = jnp.cumsum(onehot, axis=0)
    rank = jnp.take_along_axis(csum, flat_e[:, None], axis=1)[:, 0] - 1
    counts = csum[-1]
    blocks_e = (counts + bm - 1) // bm
    blocks_end = jnp.cumsum(blocks_e)
    dest = ((blocks_end - blocks_e)[flat_e] * bm + rank).astype(jnp.int32)
    n_blocks = (n * TOP_K) // bm + n_experts
    n_used = blocks_end[-1].astype(jnp.int32)
    blk_ids = jnp.arange(n_blocks, dtype=jnp.int32)
    blk = jnp.minimum(blk_ids, n_used - 1)
    blk_e = jnp.minimum(jnp.searchsorted(blocks_end, blk, side="right"), n_experts - 1).astype(jnp.int32)
    rows_left = counts[blk_e] - (blk - (blocks_end - blocks_e)[blk_e]) * bm
    live = jnp.where(blk_ids < n_used, jnp.where(rows_left > bm // 2, bm, bm // 2), 0).astype(jnp.int32)
    slot_tok = jnp.zeros((n_blocks * bm,), jnp.int32).at[dest].set(
        jnp.repeat(jnp.arange(n, dtype=jnp.int32), TOP_K))
    xg = _gather_rows(h, slot_tok, rb=2 * LANES, out_dtype=BF16)
    act = _gmm(xg, [w_gate_e, w_up_e], blk_e, live, bm=bm, bn=bn_up, kc=d, out_dtype=BF16,
               half_blocks=True, stream_w=True, name="moe_up")
    yb = _gmm(act, [w_down_e], blk_e, live, bm=bm, bn=bn_down, kc=kc_down, out_dtype=F32,
              half_blocks=True, stream_w=True, name="moe_down")
    return _combine(yb, dest, gate, tb=LANES)


def kernel(x_prompt, x_sample, c_prompt, c_sample, cache_k_a, cache_v_a, page_table, cache_k_b, cache_v_b, g_mix, g_ffn, g_final, w_ada, b_ada, w_qkv_a, w_o_a, sb_bias_a, w_qkv_b, w_o_b, sinks_b, rel_bias, w_gate_d, w_up_d, w_down_d, w_router, b_router, w_gate_e, w_up_e, w_down_e):
    bp, seq, d = x_prompt.shape
    bs, t, _ = x_sample.shape
    n_p, n_s = bp * seq, bs * t
    n = n_p + n_s
    n_heads_a = d // HEAD_DIM_A
    n_heads_b = d // HEAD_DIM_B
    depth = w_ada.shape[0]
    bm = 832 if n % 832 == 0 else 128
    assert n % bm == 0 and n_p % n_s == 0 and n % LANES == 0

    c_rows = 64
    c_all = jnp.zeros((c_rows, d), F32).at[:bp + bs].set(jnp.concatenate([c_prompt, c_sample], axis=0))
    ada = _gmm(jnp.concatenate([c_all] * depth, axis=0), [w_ada], jnp.arange(depth, dtype=jnp.int32),
               jnp.full((depth,), c_rows, jnp.int32), bm=c_rows, bn=512, kc=d, out_dtype=F32,
               pre_silu=True, name="adaln")
    ada = ada.reshape(depth, c_rows, 6, d)[:, :bp + bs] + b_ada.reshape(depth, 1, 6, d)

    def mods(layer, which):
        m = ada[layer, :, which]
        return m[:bp].reshape(bp, 1, d), jnp.repeat(m[bp:], t, axis=0).reshape(1, n_s, d)

    def norm_both(x, g, scale_of, shift_of, f=None, gate_of=None, h_dtype=BF16):
        sc_p, sc_s = mods(*scale_of)
        sh_p, sh_s = mods(*shift_of)
        gt_p, gt_s = mods(*gate_of) if f is not None else (None, None)
        res = f is not None
        xn, h = _norm(x, g, row0=0, batch=bp, rows=seq, out_rows=n, want_x=res, f=f, gate=gt_p,
                      scale=sc_p, shift=sh_p, h_dtype=h_dtype)
        xn_s, h_s = _norm(x, g, row0=n_p, batch=1, rows=n_s, out_rows=n_s, want_x=res, f=f,
                          gate=gt_s, scale=sc_s, shift=sh_s, h_dtype=h_dtype)
        xn = lax.dynamic_update_slice(xn, xn_s, (n_p, 0)) if res else x
        return xn, lax.dynamic_update_slice(h, h_s, (n_p, 0))

    x = jnp.concatenate([x_prompt.reshape(n_p, d), x_sample.reshape(n_s, d)], axis=0)

    _, h = norm_both(x, g_mix[0], (0, 1), (0, 0))
    qkv_a = _dense_mm(h, [w_qkv_a[0]], bm=bm, bn=512, kc=d, out_dtype=F32, name="qkv_a")
    kv_w = N_KV_A * HEAD_DIM_A
    o = _sb_prompt_attention(qkv_a, sb_bias_a[0].astype(F32), batch=bp, seq=seq, n_heads=n_heads_a)
    qkv_s = qkv_a[n_p:].reshape(bs, t, -1)
    k_pool = cache_k_a[0].reshape(cache_k_a.shape[1], PAGE_SIZE * N_KV_A, HEAD_DIM_A)
    v_pool = cache_v_a[0].reshape(cache_v_a.shape[1], PAGE_SIZE * N_KV_A, HEAD_DIM_A)
    o_s = _sb_sample_attention(qkv_s[..., :d], qkv_s[..., d:d + kv_w], qkv_s[..., d + kv_w:],
                               k_pool, v_pool, page_table, sb_bias_a[0])
    o = jnp.concatenate([o, o_s.reshape(n_s, d).astype(BF16)], axis=0)
    f = _dense_mm(o, [w_o_a[0]], bm=bm, bn=512, kc=d, out_dtype=F32, name="wo_a")
    x, h = norm_both(x, g_ffn[0], (0, 4), (0, 3), f=f, gate_of=(0, 2))
    act = _dense_mm(h, [w_gate_d[0], w_up_d[0]], bm=bm // 2, bn=512, kc=d, out_dtype=BF16, name="ffn_up")
    f = _dense_mm(act, [w_down_d[0]], bm=bm // 2, bn=512, kc=w_down_d.shape[1] // 2, out_dtype=F32,
                  name="ffn_down")

    x, h = norm_both(x, g_mix[1], (1, 1), (1, 0), f=f, gate_of=(0, 5))
    qkv_b = _dense_mm(h, [w_qkv_b[0]], bm=bm, bn=512, kc=d, out_dtype=F32, name="qkv_b")
    kvb_w = N_KV_B * HEAD_DIM_B
    bucket, valid = _t5_bucket_table(WINDOW, 2 * WINDOW, WINDOW)
    band_bias = rel_bias[jnp.asarray(bucket)].astype(F32).transpose(2, 0, 1)
    o = _swa_prompt_attention(qkv_b, sinks_b[0].astype(F32), band_bias,
                              jnp.asarray(valid.astype(np.float32)), batch=bp, seq=seq,
                              n_heads=n_heads_b)
    qkvb_s = qkv_b[n_p:].reshape(bs, t, -1)
    n_buf = cache_k_b.shape[2]
    kb_new, vb_new = qkvb_s[..., d:d + kvb_w], qkvb_s[..., d + kvb_w:]
    o_s = _swa_sample_attention(qkvb_s[..., :d], kb_new, vb_new,
                                cache_k_b[0].reshape(bs, n_buf, kvb_w),
                                cache_v_b[0].reshape(bs, n_buf, kvb_w), sinks_b[0], rel_bias)
    o = jnp.concatenate([o, o_s.reshape(n_s, d).astype(BF16)], axis=0)
    f = _dense_mm(o, [w_o_b[0]], bm=bm, bn=512, kc=d, out_dtype=F32, name="wo_b")
    x, h = norm_both(x, g_ffn[1], (1, 4), (1, 3), f=f, gate_of=(1, 2), h_dtype=F32)
    f = _moe(h, w_router[0], b_router[0], w_gate_e[0], w_up_e[0], w_down_e[0], bm_route=bm,
             bm=512, bn_up=512, bn_down=512, kc_down=w_down_e.shape[2] // 2)

    gt_p, gt_s = mods(1, 5)
    _, y_p = _norm(x, g_final, row0=0, batch=bp, rows=seq, out_rows=n_p, f=f, gate=gt_p, h_dtype=F32)
    _, y_s = _norm(x, g_final, row0=n_p, batch=1, rows=n_s, out_rows=n_s, f=f, gate=gt_s, h_dtype=F32)

    pages = seq // PAGE_SIZE
    k_a_prompt = qkv_a[:n_p, d:d + kv_w].reshape(1, bp, pages, PAGE_SIZE, N_KV_A, HEAD_DIM_A)
    v_a_prompt = qkv_a[:n_p, d + kv_w:].reshape(1, bp, pages, PAGE_SIZE, N_KV_A, HEAD_DIM_A)
    k_a_sample = qkv_s[..., d:d + kv_w].reshape(1, bs, t, N_KV_A, HEAD_DIM_A)
    v_a_sample = qkv_s[..., d + kv_w:].reshape(1, bs, t, N_KV_A, HEAD_DIM_A)
    n_keep = min(WINDOW, seq)

    def last_rows(c0):
        return jnp.stack([qkv_b[(i + 1) * seq - n_keep:(i + 1) * seq, c0:c0 + kvb_w] for i in range(bp)]
                         ).reshape(1, bp, n_keep, N_KV_B, HEAD_DIM_B)

    k_b_prompt, v_b_prompt = last_rows(d), last_rows(d + kvb_w)
    k_b_sample = jnp.concatenate([cache_k_b[0][:, t:], kb_new.reshape(bs, t, N_KV_B, HEAD_DIM_B)], axis=1)[None]
    v_b_sample = jnp.concatenate([cache_v_b[0][:, t:], vb_new.reshape(bs, t, N_KV_B, HEAD_DIM_B)], axis=1)[None]
    return (y_p.reshape(bp, seq, d), y_s.reshape(bs, t, d), k_a_prompt, v_a_prompt, k_a_sample,
            v_a_sample, k_b_prompt, v_b_prompt, k_b_sample, v_b_sample)
```
